```python
import math
import jax, jax.numpy as jnp
from jax import lax
import numpy as np

D_MODEL = 1024
BATCH = 16
SEQ = 4096
DEPTH = 1
DEC_BATCH = 128
DEC_SEQ = 1
PAST_LEN = 8192
PAGE_SIZE = 128

NSA_HEADS = 8
NSA_KV = 2
NSA_HPG = NSA_HEADS // NSA_KV
HEAD_DIM = 64
CMP_LEN = 32
CMP_STRIDE = 16
CMP_HID = 128
SLC_LEN = 64
SLC_TOP = 16
WINDOW = 512
NSA_QBLK = 32
FORCE_SCORE = 1.0e6
SB_HEADS = 8
SB_QBLK = 128
MEM_LEN = 256
MEM_HEADS = 4
MEM_HD = 128
REL_BUCKETS = 32
REL_MAX_DIST = 2048
N_EXPERTS = 32
TOP_K = 4
D_FF = 1024
SWIGLU_LIMIT = 7.0
SWIGLU_ALPHA = 1.702
MOE_BLOCK = 256
EPS = 1e-6
NEG_INF = -1.0e30

IN_WIDTH = (NSA_HEADS * HEAD_DIM + 6 * NSA_KV * HEAD_DIM + 3 * NSA_HEADS
            + 3 * SB_HEADS * HEAD_DIM + MEM_HEADS * MEM_HD + 3 * D_MODEL)

kernel_name = 'nsa_stickbreak_memx_moe_step'


def rmsnorm(x, g):
    xf = x.astype(jnp.float32)
    xf = xf * lax.rsqrt(jnp.mean(xf * xf, axis=-1, keepdims=True) + EPS)
    return (xf * g.astype(jnp.float32)).astype(x.dtype)


def masked_softmax(s, mask):
    p = jax.nn.softmax(jnp.where(mask, s, NEG_INF), axis=-1)
    return jnp.where(mask, p, 0.0)


def rel_bucket(dist):
    n = jnp.maximum(dist, 0)
    max_exact = REL_BUCKETS // 2
    nf = jnp.maximum(n, 1).astype(jnp.float32)
    large = max_exact + (jnp.log(nf / max_exact) / math.log(REL_MAX_DIST / max_exact)
                         * (REL_BUCKETS - max_exact)).astype(jnp.int32)
    return jnp.where(n < max_exact, n, jnp.minimum(large, REL_BUCKETS - 1))


def split_in(h, w_in):
    B, T = h.shape[:2]
    z = h @ w_in
    sizes = (NSA_HEADS * HEAD_DIM, 6 * NSA_KV * HEAD_DIM, 3 * NSA_HEADS,
             3 * SB_HEADS * HEAD_DIM, MEM_HEADS * MEM_HD, 3 * D_MODEL)
    offs = np.cumsum(sizes)[:-1].tolist()
    q_n, kv_n, g_n, qkv_s, q_m, g_m = jnp.split(z, offs, axis=-1)
    return (q_n.reshape(B, T, NSA_KV, NSA_HPG, HEAD_DIM),
            kv_n.reshape(B, T, 6, NSA_KV, HEAD_DIM),
            jax.nn.sigmoid(g_n).reshape(B, T, NSA_KV, NSA_HPG, 3),
            qkv_s.reshape(B, T, 3, SB_HEADS, HEAD_DIM),
            q_m.reshape(B, T, MEM_HEADS, MEM_HD),
            jax.nn.sigmoid(g_m).reshape(B, T, 3, D_MODEL))


def compress(x, pe, w1, w2):
    B, T = x.shape[:2]
    R = CMP_LEN // CMP_STRIDE
    nh = T // CMP_STRIDE
    nc = nh - R + 1
    halves = x[:, :nh * CMP_STRIDE].reshape(B, nh, CMP_STRIDE, NSA_KV, HEAD_DIM)
    w1h = w1.reshape(R, CMP_STRIDE, HEAD_DIM, CMP_HID)
    pre = pe.reshape(-1) @ w1
    for r in range(R):
        pre = pre + jnp.einsum('bnsgd,sdh->bngh', halves[:, r:r + nc], w1h[r])
    return jax.nn.gelu(pre) @ w2


def to_blocks(x):
    B, T = x.shape[:2]
    ns = -(-T // SLC_LEN)
    x = jnp.pad(x, ((0, 0), (0, ns * SLC_LEN - T), (0, 0), (0, 0)))
    return x.reshape(B, ns, SLC_LEN, NSA_KV, HEAD_DIM).transpose(0, 3, 1, 2, 4)


def nsa_attend(q, q_pos, gates, kc, vc, ks_blk, vs_blk, kw, vw, kw_pos, rel_bias):
    B, Q = q.shape[:2]
    NC = kc.shape[1]
    NS = ks_blk.shape[2]
    KW = kw.shape[1]
    scale = HEAD_DIM ** -0.5
    f32 = jnp.float32
    c_start = jnp.arange(NC, dtype=jnp.int32) * CMP_STRIDE
    c_end = c_start + (CMP_LEN - 1)
    c_mask = (c_end[None, :] <= q_pos[:, None])[:, None, None, :]
    c_bias = rel_bias[rel_bucket(q_pos[:, None] - c_end[None, :])].reshape(
        Q, NC, NSA_KV, NSA_HPG).transpose(0, 2, 3, 1)
    s_c = jnp.einsum('bqghd,bcgd->bqghc', q, kc, preferred_element_type=f32) * scale + c_bias
    p_c = masked_softmax(s_c, c_mask)
    o_c = jnp.einsum('bqghc,bcgd->bqghd', p_c.astype(vc.dtype), vc)
    s_start = jnp.arange(NS, dtype=jnp.int32) * SLC_LEN
    overlap = ((c_start[:, None] < s_start[None, :] + SLC_LEN)
               & (c_end[:, None] >= s_start[None, :])).astype(f32)
    imp = jnp.einsum('bqghc,cj->bqgj', p_c, overlap)
    blk = jnp.arange(NS, dtype=jnp.int32)[None, :]
    cur = (q_pos // SLC_LEN)[:, None]
    forced = (blk == 0) | (blk == cur) | (blk == cur - 1)
    valid = s_start[None, :] <= q_pos[:, None]
    imp = jnp.where(forced[None, :, None, :], FORCE_SCORE, imp)
    imp = jnp.where(valid[None, :, None, :], imp, -1.0)
    n_top = min(SLC_TOP, NS)
    top_val, idx = lax.top_k(imp, n_top)
    bi = jnp.arange(B)[:, None, None, None]
    gi = jnp.arange(NSA_KV)[None, None, :, None]
    k_sel = ks_blk[bi, gi, idx]
    v_sel = vs_blk[bi, gi, idx]
    k_pos = idx[..., None] * SLC_LEN + jnp.arange(SLC_LEN, dtype=jnp.int32)
    qp = q_pos[None, :, None, None, None]
    s_mask = (top_val >= 0)[..., None] & (k_pos <= qp)
    bias_g = rel_bias.reshape(REL_BUCKETS, NSA_KV, NSA_HPG).transpose(1, 0, 2)
    s_bias = bias_g[gi[..., None], rel_bucket(qp - k_pos)]
    n_keys = n_top * SLC_LEN
    s_bias = s_bias.transpose(0, 1, 2, 5, 3, 4).reshape(B, Q, NSA_KV, NSA_HPG, n_keys)
    s_s = jnp.einsum('bqghd,bqgnkd->bqghnk', q, k_sel, preferred_element_type=f32).reshape(
        B, Q, NSA_KV, NSA_HPG, n_keys) * scale + s_bias
    p_s = masked_softmax(s_s, s_mask.reshape(B, Q, NSA_KV, 1, n_keys))
    o_s = jnp.einsum('bqghk,bqgkd->bqghd', p_s.astype(v_sel.dtype),
                     v_sel.reshape(B, Q, NSA_KV, n_keys, HEAD_DIM))
    w_dist = q_pos[:, None] - kw_pos[None, :]
    w_mask = ((w_dist >= 0) & (w_dist < WINDOW) & (kw_pos[None, :] >= 0))[:, None, None, :]
    w_bias = rel_bias[rel_bucket(w_dist)].reshape(Q, KW, NSA_KV, NSA_HPG).transpose(0, 2, 3, 1)
    s_w = jnp.einsum('bqghd,bkgd->bqghk', q, kw, preferred_element_type=f32) * scale + w_bias
    p_w = masked_softmax(s_w, w_mask)
    o_w = jnp.einsum('bqghk,bkgd->bqghd', p_w.astype(vw.dtype), vw)
    g = gates.astype(o_c.dtype)
    o = g[..., 0:1] * o_c + g[..., 1:2] * o_s + g[..., 2:3] * o_w
    return o.reshape(B, Q, NSA_HEADS * HEAD_DIM)


def nsa_prompt(q, gates, kc, vc, ks_blk, vs_blk, k_win, v_win, rel_bias):
    B, T = q.shape[:2]
    pad = ((0, 0), (WINDOW, 0), (0, 0), (0, 0))
    kw_pad = jnp.pad(k_win, pad)
    vw_pad = jnp.pad(v_win, pad)

    def block(i):
        t0 = i * NSA_QBLK
        qb = lax.dynamic_slice_in_dim(q, t0, NSA_QBLK, axis=1)
        gb = lax.dynamic_slice_in_dim(gates, t0, NSA_QBLK, axis=1)
        kw = lax.dynamic_slice_in_dim(kw_pad, t0, WINDOW + NSA_QBLK, axis=1)
        vw = lax.dynamic_slice_in_dim(vw_pad, t0, WINDOW + NSA_QBLK, axis=1)
        q_pos = t0 + jnp.arange(NSA_QBLK, dtype=jnp.int32)
        kw_pos = t0 - WINDOW + jnp.arange(WINDOW + NSA_QBLK, dtype=jnp.int32)
        return nsa_attend(qb, q_pos, gb, kc, vc, ks_blk, vs_blk, kw, vw, kw_pos, rel_bias)

    out = lax.map(block, jnp.arange(T // NSA_QBLK, dtype=jnp.int32))
    return out.transpose(1, 0, 2, 3).reshape(B, T, NSA_HEADS * HEAD_DIM)


def sb_attend(q, q_pos, k, v, k_pos):
    z = jnp.einsum('bqhd,bkhd->bhqk', q, k, preferred_element_type=jnp.float32) * (HEAD_DIM ** -0.5)
    causal = k_pos[None, :] < q_pos[:, None]
    log_1m = jnp.where(causal, -jax.nn.softplus(z), 0.0)
    surv = lax.cumsum(log_1m, axis=3, reverse=True) - log_1m
    a = jnp.where(causal, jnp.exp(jax.nn.log_sigmoid(z) + surv), 0.0)
    return jnp.einsum('bhqk,bkhd->bqhd', a.astype(v.dtype), v)


def sb_prompt(q, k, v):
    B, T = q.shape[:2]
    k_pos = jnp.arange(T, dtype=jnp.int32)

    def block(i):
        t0 = i * SB_QBLK
        qb = lax.dynamic_slice_in_dim(q, t0, SB_QBLK, axis=1)
        return sb_attend(qb, t0 + jnp.arange(SB_QBLK, dtype=jnp.int32), k, v, k_pos)

    out = lax.map(block, jnp.arange(T // SB_QBLK, dtype=jnp.int32))
    return out.transpose(1, 0, 2, 3, 4).reshape(B, T, SB_HEADS * HEAD_DIM)


def mem_attend(q, mk, mv):
    s = jnp.einsum('bqhd,bmhd->bhqm', q, mk, preferred_element_type=jnp.float32) * (MEM_HD ** -0.5)
    p = jax.nn.softmax(s, axis=-1)
    return jnp.einsum('bhqm,bmhd->bqhd', p.astype(mv.dtype), mv)


def merge_out(o_n, o_s, o_m, g_m, p):
    y = (g_m[:, :, 0] * (o_n @ p['w_br_nsa'])
         + g_m[:, :, 1] * (o_s @ p['w_br_sb'])
         + g_m[:, :, 2] * (o_m @ p['w_br_mem']))
    return y @ p['w_out']


def moe(h, w_router, b_router, w_up, b_up, w_down, b_down):
    lead = h.shape[:-1]
    xf = h.reshape(-1, D_MODEL)
    N = xf.shape[0]
    logits = jnp.einsum('nd,de->ne', xf, w_router, preferred_element_type=jnp.float32) + b_router.astype(jnp.float32)
    top_l, top_e = lax.top_k(logits, TOP_K)
    top_w = jax.nn.softmax(top_l, axis=-1)
    n_assign = N * TOP_K
    blk = int(min(MOE_BLOCK, max(8, n_assign // N_EXPERTS)))
    n_rows = (n_assign // blk + N_EXPERTS) * blk
    n_blocks = n_rows // blk
    flat_e = top_e.reshape(-1)
    flat_t = jnp.arange(n_assign, dtype=jnp.int32) // TOP_K
    flat_w = top_w.reshape(-1)
    order = jnp.argsort(flat_e)
    se = flat_e[order]
    counts = jnp.bincount(flat_e, length=N_EXPERTS)
    start = jnp.cumsum(counts) - counts
    padded = (counts + blk - 1) // blk * blk
    p_end = jnp.cumsum(padded)
    p_start = p_end - padded
    dest = p_start[se] + jnp.arange(n_assign) - start[se]
    row_tok = jnp.full((n_rows,), N, jnp.int32).at[dest].set(flat_t[order])
    row_w = jnp.zeros((n_rows,), jnp.float32).at[dest].set(flat_w[order])
    blk_e = jnp.minimum(jnp.searchsorted(p_end, jnp.arange(n_blocks) * blk, side='right'), N_EXPERTS - 1)
    x_pad = jnp.concatenate([xf, jnp.zeros((1, D_MODEL), xf.dtype)], axis=0)

    def run(args):
        e, toks = args
        gu = x_pad[toks] @ w_up[e] + b_up[e]
        gate = jnp.minimum(gu[:, :D_FF], SWIGLU_LIMIT)
        up = jnp.clip(gu[:, D_FF:], -SWIGLU_LIMIT, SWIGLU_LIMIT)
        act = (up + 1.0) * gate * jax.nn.sigmoid(SWIGLU_ALPHA * gate)
        return act @ w_down[e] + b_down[e]

    out = lax.map(run, (blk_e, row_tok.reshape(n_blocks, blk)))
    out = out.reshape(n_rows, D_MODEL) * row_w[:, None].astype(out.dtype)
    y = jnp.zeros((N + 1, D_MODEL), out.dtype).at[row_tok].add(out)[:N]
    return y.reshape(*lead, D_MODEL)


def ffn_sublayer(x, p):
    return x + moe(rmsnorm(x, p['g_ffn']), p['w_router'], p['b_router'],
                   p['w_up'], p['b_up'], p['w_down'], p['b_down'])


def prompt_layer(x, mem, rel_bias, p):
    B, T = x.shape[:2]
    h = rmsnorm(x, p['g_attn'])
    q_n, kv_n, g_n, qkv_s, q_m, g_m = split_in(h, p['w_in'])
    kc = compress(kv_n[:, :, 0], p['cmp_pe_k'], p['cmp_w1_k'], p['cmp_w2_k'])
    vc = compress(kv_n[:, :, 1], p['cmp_pe_v'], p['cmp_w1_v'], p['cmp_w2_v'])
    o_n = nsa_prompt(q_n, g_n, kc, vc, to_blocks(kv_n[:, :, 2]), to_blocks(kv_n[:, :, 3]),
                     kv_n[:, :, 4], kv_n[:, :, 5], rel_bias)
    o_s = sb_prompt(qkv_s[:, :, 0], qkv_s[:, :, 1], qkv_s[:, :, 2])
    mem_kv = (rmsnorm(mem, p['g_mem']) @ p['w_mem_kv']).reshape(B, mem.shape[1], 2, MEM_HEADS, MEM_HD)
    o_m = mem_attend(q_m, mem_kv[:, :, 0], mem_kv[:, :, 1]).reshape(B, T, MEM_HEADS * MEM_HD)
    x = x + merge_out(o_n, o_s, o_m, g_m, p)
    x = ffn_sublayer(x, p)
    wb = min(WINDOW, T)
    return x, (kv_n[:, :, 0:2], kv_n[:, :, 2:4], qkv_s[:, :, 1:3], kv_n[:, T - wb:, 4:6], mem_kv)


def sample_layer(x, c_cmp, c_slc, c_sb, c_win, c_mem, page_table, rel_bias, p):
    B, S = x.shape[:2]
    past = page_table.shape[1] * PAGE_SIZE
    h = rmsnorm(x, p['g_attn'])
    q_n, kv_n, g_n, qkv_s, q_m, g_m = split_in(h, p['w_in'])
    wb = c_win.shape[1]
    win_buf = jnp.concatenate([c_win, kv_n[:, :, 4:6]], axis=1)
    q_pos = past + jnp.arange(S, dtype=jnp.int32)
    kw_pos = past - wb + jnp.arange(wb + S, dtype=jnp.int32)
    k_pos = jnp.arange(past + S, dtype=jnp.int32)

    def per_seq(args):
        pt, qn, gn, qs, new_n, new_s, wbuf = args
        cmp_rows = jnp.concatenate([c_cmp[pt].reshape(past, 2, NSA_KV, HEAD_DIM), new_n[:, 0:2]], axis=0)[None]
        slc_rows = jnp.concatenate([c_slc[pt].reshape(past, 2, NSA_KV, HEAD_DIM), new_n[:, 2:4]], axis=0)[None]
        sb_rows = jnp.concatenate([c_sb[pt].reshape(past, 2, SB_HEADS, HEAD_DIM), new_s], axis=0)[None]
        kc = compress(cmp_rows[:, :, 0], p['cmp_pe_k'], p['cmp_w1_k'], p['cmp_w2_k'])
        vc = compress(cmp_rows[:, :, 1], p['cmp_pe_v'], p['cmp_w1_v'], p['cmp_w2_v'])
        o_n = nsa_attend(qn[None], q_pos, gn[None], kc, vc, to_blocks(slc_rows[:, :, 0]),
                         to_blocks(slc_rows[:, :, 1]), wbuf[None, :, 0], wbuf[None, :, 1], kw_pos, rel_bias)
        o_s = sb_attend(qs[None], q_pos, sb_rows[:, :, 0], sb_rows[:, :, 1], k_pos)
        return o_n[0], o_s[0].reshape(S, SB_HEADS * HEAD_DIM)

    o_n, o_s = lax.map(per_seq, (page_table, q_n, g_n, qkv_s[:, :, 0], kv_n, qkv_s[:, :, 1:3], win_buf))
    o_m = mem_attend(q_m, c_mem[:, :, 0], c_mem[:, :, 1]).reshape(B, S, MEM_HEADS * MEM_HD)
    x = x + merge_out(o_n, o_s, o_m, g_m, p)
    x = ffn_sublayer(x, p)
    return x, (kv_n[:, :, 0:2], kv_n[:, :, 2:4], qkv_s[:, :, 1:3], win_buf[:, S:])


def setup_inputs(seed: int = 0) -> dict:
    key = jax.random.key(seed)
    keys = iter(jax.random.split(key, 64))

    def nrm(shape, scale):
        r = jax.random.normal(next(keys), shape, jnp.float32)
        return r if scale == 1.0 else r * scale

    def gain(shape):
        return 1.0 + nrm(shape, 0.02)

    n_pages = PAST_LEN // PAGE_SIZE
    n_used = DEC_BATCH * n_pages
    n_phys = n_used + (n_used + 3) // 4
    wb = min(WINDOW, PAST_LEN)
    L = DEPTH
    page_table = jax.random.permutation(next(keys), n_phys)[:n_used].reshape(DEC_BATCH, n_pages).astype(jnp.int32)
    return {
        'x_prompt': nrm((BATCH, SEQ, D_MODEL), 1.0),
        'x_sample': nrm((DEC_BATCH, DEC_SEQ, D_MODEL), 1.0),
        'mem_prompt': nrm((BATCH, MEM_LEN, D_MODEL), 1.0),
        'cache_nsa_cmp': nrm((L, n_phys, PAGE_SIZE, 2, NSA_KV, HEAD_DIM), 1.0),
        'cache_nsa_slc': nrm((L, n_phys, PAGE_SIZE, 2, NSA_KV, HEAD_DIM), 1.0),
        'cache_sb': nrm((L, n_phys, PAGE_SIZE, 2, SB_HEADS, HEAD_DIM), 1.0),
        'cache_nsa_win': nrm((L, DEC_BATCH, wb, 2, NSA_KV, HEAD_DIM), 1.0),
        'cache_mem': nrm((L, DEC_BATCH, MEM_LEN, 2, MEM_HEADS, MEM_HD), 1.0),
        'page_table': page_table,
        'rel_bias': nrm((REL_BUCKETS, NSA_HEADS), 0.5),
        'g_attn': gain((L, D_MODEL)),
        'w_in': nrm((L, D_MODEL, IN_WIDTH), D_MODEL ** -0.5),
        'cmp_pe_k': nrm((L, CMP_LEN, HEAD_DIM), 0.5),
        'cmp_w1_k': nrm((L, CMP_LEN * HEAD_DIM, CMP_HID), (CMP_LEN * HEAD_DIM) ** -0.5),
        'cmp_w2_k': nrm((L, CMP_HID, HEAD_DIM), CMP_HID ** -0.5),
        'cmp_pe_v': nrm((L, CMP_LEN, HEAD_DIM), 0.5),
        'cmp_w1_v': nrm((L, CMP_LEN * HEAD_DIM, CMP_HID), (CMP_LEN * HEAD_DIM) ** -0.5),
        'cmp_w2_v': nrm((L, CMP_HID, HEAD_DIM), CMP_HID ** -0.5),
        'g_mem': gain((L, D_MODEL)),
        'w_mem_kv': nrm((L, D_MODEL, 2 * MEM_HEADS * MEM_HD), D_MODEL ** -0.5),
        'w_br_nsa': nrm((L, NSA_HEADS * HEAD_DIM, D_MODEL), (NSA_HEADS * HEAD_DIM) ** -0.5),
        'w_br_sb': nrm((L, SB_HEADS * HEAD_DIM, D_MODEL), (SB_HEADS * HEAD_DIM) ** -0.5),
        'w_br_mem': nrm((L, MEM_HEADS * MEM_HD, D_MODEL), (MEM_HEADS * MEM_HD) ** -0.5),
        'w_out': nrm((L, D_MODEL, D_MODEL), D_MODEL ** -0.5),
        'g_ffn': gain((L, D_MODEL)),
        'w_router': nrm((L, D_MODEL, N_EXPERTS), D_MODEL ** -0.5),
        'b_router': nrm((L, N_EXPERTS), 0.01),
        'w_up': nrm((L, N_EXPERTS, D_MODEL, 2 * D_FF), D_MODEL ** -0.5),
        'b_up': nrm((L, N_EXPERTS, 2 * D_FF), 0.01),
        'w_down': nrm((L, N_EXPERTS, D_FF, D_MODEL), D_FF ** -0.5),
        'b_down': nrm((L, N_EXPERTS, D_MODEL), 0.01),
        'g_final': gain((D_MODEL,)),
    }


def reference(x_prompt, x_sample, mem_prompt, cache_nsa_cmp, cache_nsa_slc, cache_sb, cache_nsa_win,
              cache_mem, page_table, rel_bias, g_attn, w_in, cmp_pe_k, cmp_w1_k, cmp_w2_k, cmp_pe_v,
              cmp_w1_v, cmp_w2_v, g_mem, w_mem_kv, w_br_nsa, w_br_sb, w_br_mem, w_out, g_ffn, w_router,
              b_router, w_up, b_up, w_down, b_down, g_final):
    xp, xs = x_prompt, x_sample
    st_p, st_s = [], []
    for l in range(DEPTH):
        p = {'g_attn': g_attn[l], 'w_in': w_in[l],
             'cmp_pe_k': cmp_pe_k[l], 'cmp_w1_k': cmp_w1_k[l], 'cmp_w2_k': cmp_w2_k[l],
             'cmp_pe_v': cmp_pe_v[l], 'cmp_w1_v': cmp_w1_v[l], 'cmp_w2_v': cmp_w2_v[l],
             'g_mem': g_mem[l], 'w_mem_kv': w_mem_kv[l],
             'w_br_nsa': w_br_nsa[l], 'w_br_sb': w_br_sb[l], 'w_br_mem': w_br_mem[l], 'w_out': w_out[l],
             'g_ffn': g_ffn[l], 'w_router': w_router[l], 'b_router': b_router[l],
             'w_up': w_up[l], 'b_up': b_up[l], 'w_down': w_down[l], 'b_down': b_down[l]}
        xp, sp = prompt_layer(xp, mem_prompt, rel_bias, p)
        xs, ss = sample_layer(xs, cache_nsa_cmp[l], cache_nsa_slc[l], cache_sb[l], cache_nsa_win[l],
                              cache_mem[l], page_table, rel_bias, p)
        st_p.append(sp)
        st_s.append(ss)
    y_prompt = rmsnorm(xp, g_final)
    y_sample = rmsnorm(xs, g_final)

    def stacked(states, i):
        return jnp.stack([s[i] for s in states])

    return (y_prompt, y_sample,
            stacked(st_p, 0), stacked(st_p, 1), stacked(st_p, 2), stacked(st_p, 3), stacked(st_p, 4),
            stacked(st_s, 0), stacked(st_s, 1), stacked(st_s, 2), stacked(st_s, 3))
```

```python
import functools
import math

import jax
import jax.numpy as jnp
import numpy as np
from jax import lax
from jax.experimental import pallas as pl
from jax.experimental.pallas import tpu as pltpu

D_MODEL = 1024
PAGE_SIZE = 128
NSA_HEADS = 8
NSA_KV = 2
NSA_HPG = NSA_HEADS // NSA_KV
HEAD_DIM = 64
CMP_LEN = 32
CMP_STRIDE = 16
CMP_HID = 128
SLC_LEN = 64
SLC_TOP = 16
WINDOW = 512
NSA_QBLK = 32
FORCE_SCORE = 1.0e6
SB_HEADS = 8
SB_QBLK = 128
MEM_HEADS = 4
MEM_HD = 128
REL_BUCKETS = 32
REL_MAX_DIST = 2048
N_EXPERTS = 32
TOP_K = 4
D_FF = 1024
SWIGLU_LIMIT = 7.0
SWIGLU_ALPHA = 1.702
MOE_BLOCK = 256
EPS = 1e-6
NEG_INF = -1.0e30

SEG_QN = NSA_HEADS * HEAD_DIM
SEG_KVN = 6 * NSA_KV * HEAD_DIM
SEG_GN = 3 * NSA_HEADS
SEG_QKVS = 3 * SB_HEADS * HEAD_DIM
SEG_QM = MEM_HEADS * MEM_HD
SEG_GM = 3 * D_MODEL
IN_SIZES = (SEG_QN, SEG_KVN, SEG_GN, SEG_QKVS, SEG_QM, SEG_GM)

VMEM_LIMIT = 48 * 1024 * 1024


def _rms_proj_kernel(x_ref, g_ref, *refs, n_seg, sigmoid):
    w_refs = refs[:n_seg]
    o_refs = refs[n_seg:]
    x = x_ref[...]
    h = x * lax.rsqrt(jnp.mean(x * x, axis=-1, keepdims=True) + EPS) * g_ref[...]
    hb = h.astype(jnp.bfloat16)
    for w_ref, o_ref, sig in zip(w_refs, o_refs, sigmoid):
        z = jnp.dot(hb, w_ref[...], preferred_element_type=jnp.float32)
        if sig:
            z = jax.nn.sigmoid(z)
        o_ref[...] = z.astype(o_ref.dtype)


def rms_proj(x2d, g, weights, sigmoid, out_dtypes, tm):
    n, d = x2d.shape
    assert n % tm == 0
    n_seg = len(weights)
    in_specs = [pl.BlockSpec((tm, d), lambda i: (i, 0)),
                pl.BlockSpec((1, d), lambda i: (0, 0))]
    for w in weights:
        in_specs.append(pl.BlockSpec(w.shape, lambda i: (0, 0), pipeline_mode=pl.Buffered(1)))
    out_specs = [pl.BlockSpec((tm, w.shape[1]), lambda i: (i, 0)) for w in weights]
    out_shape = [jax.ShapeDtypeStruct((n, w.shape[1]), dt) for w, dt in zip(weights, out_dtypes)]
    return pl.pallas_call(
        functools.partial(_rms_proj_kernel, n_seg=n_seg, sigmoid=tuple(sigmoid)),
        grid=(n // tm,),
        in_specs=in_specs,
        out_specs=out_specs,
        out_shape=out_shape,
        compiler_params=pltpu.CompilerParams(dimension_semantics=("parallel",),
                                             vmem_limit_bytes=VMEM_LIMIT),
        name="rms_proj",
    )(x2d, g.reshape(1, d), *weights)


def in_proj(x, g_attn, w_segs):
    B, T, D = x.shape
    n = B * T
    tm = 256 if n % 256 == 0 else n
    outs = rms_proj(x.reshape(n, D), g_attn, w_segs,
                    sigmoid=(False, False, True, False, False, True),
                    out_dtypes=(jnp.float32,) * 6, tm=tm)
    q_n, kv_n, g_n, qkv_s, q_m, g_m = outs
    return (q_n.reshape(B, T, NSA_KV, NSA_HPG, HEAD_DIM),
            kv_n.reshape(B, T, 6, NSA_KV, HEAD_DIM),
            g_n.reshape(B, T, NSA_KV, NSA_HPG, 3),
            qkv_s.reshape(B, T, 3, SB_HEADS, HEAD_DIM),
            q_m.reshape(B, T, MEM_HEADS, MEM_HD),
            g_m.reshape(B, T, 3, D_MODEL))


def split_w_in(w_in):
    offs = np.cumsum((0,) + IN_SIZES)
    return [w_in[:, int(a):int(b)].astype(jnp.bfloat16) for a, b in zip(offs[:-1], offs[1:])]


def rmsnorm(x, g):
    xf = x.astype(jnp.float32)
    xf = xf * lax.rsqrt(jnp.mean(xf * xf, axis=-1, keepdims=True) + EPS)
    return (xf * g.astype(jnp.float32)).astype(x.dtype)


def masked_softmax(s, mask):
    p = jax.nn.softmax(jnp.where(mask, s, NEG_INF), axis=-1)
    return jnp.where(mask, p, 0.0)


def rel_bucket(dist):
    n = jnp.maximum(dist, 0)
    max_exact = REL_BUCKETS // 2
    nf = jnp.maximum(n, 1).astype(jnp.float32)
    large = max_exact + (jnp.log(nf / max_exact) / math.log(REL_MAX_DIST / max_exact)
                         * (REL_BUCKETS - max_exact)).astype(jnp.int32)
    return jnp.where(n < max_exact, n, jnp.minimum(large, REL_BUCKETS - 1))


def compress(x, pe, w1, w2):
    B, T = x.shape[:2]
    R = CMP_LEN // CMP_STRIDE
    nh = T // CMP_STRIDE
    nc = nh - R + 1
    halves = x[:, :nh * CMP_STRIDE].reshape(B, nh, CMP_STRIDE, NSA_KV, HEAD_DIM)
    w1h = w1.reshape(R, CMP_STRIDE, HEAD_DIM, CMP_HID)
    pre = pe.reshape(-1) @ w1
    for r in range(R):
        pre = pre + jnp.einsum('bnsgd,sdh->bngh', halves[:, r:r + nc], w1h[r])
    return jax.nn.gelu(pre) @ w2


def to_blocks(x):
    B, T = x.shape[:2]
    ns = -(-T // SLC_LEN)
    x = jnp.pad(x, ((0, 0), (0, ns * SLC_LEN - T), (0, 0), (0, 0)))
    return x.reshape(B, ns, SLC_LEN, NSA_KV, HEAD_DIM).transpose(0, 3, 1, 2, 4)


def nsa_attend(q, q_pos, gates, kc, vc, ks_blk, vs_blk, kw, vw, kw_pos, rel_bias):
    B, Q = q.shape[:2]
    NC = kc.shape[1]
    NS = ks_blk.shape[2]
    KW = kw.shape[1]
    scale = HEAD_DIM ** -0.5
    f32 = jnp.float32
    c_start = jnp.arange(NC, dtype=jnp.int32) * CMP_STRIDE
    c_end = c_start + (CMP_LEN - 1)
    c_mask = (c_end[None, :] <= q_pos[:, None])[:, None, None, :]
    c_bias = rel_bias[rel_bucket(q_pos[:, None] - c_end[None, :])].reshape(
        Q, NC, NSA_KV, NSA_HPG).transpose(0, 2, 3, 1)
    s_c = jnp.einsum('bqghd,bcgd->bqghc', q, kc, preferred_element_type=f32) * scale + c_bias
    p_c = masked_softmax(s_c, c_mask)
    o_c = jnp.einsum('bqghc,bcgd->bqghd', p_c.astype(vc.dtype), vc)
    s_start = jnp.arange(NS, dtype=jnp.int32) * SLC_LEN
    overlap = ((c_start[:, None] < s_start[None, :] + SLC_LEN)
               & (c_end[:, None] >= s_start[None, :])).astype(f32)
    imp = jnp.einsum('bqghc,cj->bqgj', p_c, overlap)
    blk = jnp.arange(NS, dtype=jnp.int32)[None, :]
    cur = (q_pos // SLC_LEN)[:, None]
    forced = (blk == 0) | (blk == cur) | (blk == cur - 1)
    valid = s_start[None, :] <= q_pos[:, None]
    imp = jnp.where(forced[None, :, None, :], FORCE_SCORE, imp)
    imp = jnp.where(valid[None, :, None, :], imp, -1.0)
    n_top = min(SLC_TOP, NS)
    top_val, idx = lax.top_k(imp, n_top)
    bi = jnp.arange(B)[:, None, None, None]
    gi = jnp.arange(NSA_KV)[None, None, :, None]
    k_sel = ks_blk[bi, gi, idx]
    v_sel = vs_blk[bi, gi, idx]
    k_pos = idx[..., None] * SLC_LEN + jnp.arange(SLC_LEN, dtype=jnp.int32)
    qp = q_pos[None, :, None, None, None]
    s_mask = (top_val >= 0)[..., None] & (k_pos <= qp)
    bias_g = rel_bias.reshape(REL_BUCKETS, NSA_KV, NSA_HPG).transpose(1, 0, 2)
    s_bias = bias_g[gi[..., None], rel_bucket(qp - k_pos)]
    n_keys = n_top * SLC_LEN
    s_bias = s_bias.transpose(0, 1, 2, 5, 3, 4).reshape(B, Q, NSA_KV, NSA_HPG, n_keys)
    s_s = jnp.einsum('bqghd,bqgnkd->bqghnk', q, k_sel, preferred_element_type=f32).reshape(
        B, Q, NSA_KV, NSA_HPG, n_keys) * scale + s_bias
    p_s = masked_softmax(s_s, s_mask.reshape(B, Q, NSA_KV, 1, n_keys))
    o_s = jnp.einsum('bqghk,bqgkd->bqghd', p_s.astype(v_sel.dtype),
                     v_sel.reshape(B, Q, NSA_KV, n_keys, HEAD_DIM))
    w_dist = q_pos[:, None] - kw_pos[None, :]
    w_mask = ((w_dist >= 0) & (w_dist < WINDOW) & (kw_pos[None, :] >= 0))[:, None, None, :]
    w_bias = rel_bias[rel_bucket(w_dist)].reshape(Q, KW, NSA_KV, NSA_HPG).transpose(0, 2, 3, 1)
    s_w = jnp.einsum('bqghd,bkgd->bqghk', q, kw, preferred_element_type=f32) * scale + w_bias
    p_w = masked_softmax(s_w, w_mask)
    o_w = jnp.einsum('bqghk,bkgd->bqghd', p_w.astype(vw.dtype), vw)
    g = gates.astype(o_c.dtype)
    o = g[..., 0:1] * o_c + g[..., 1:2] * o_s + g[..., 2:3] * o_w
    return o.reshape(B, Q, NSA_HEADS * HEAD_DIM)


def nsa_prompt(q, gates, kc, vc, ks_blk, vs_blk, k_win, v_win, rel_bias):
    B, T = q.shape[:2]
    pad = ((0, 0), (WINDOW, 0), (0, 0), (0, 0))
    kw_pad = jnp.pad(k_win, pad)
    vw_pad = jnp.pad(v_win, pad)

    def block(i):
        t0 = i * NSA_QBLK
        qb = lax.dynamic_slice_in_dim(q, t0, NSA_QBLK, axis=1)
        gb = lax.dynamic_slice_in_dim(gates, t0, NSA_QBLK, axis=1)
        kw = lax.dynamic_slice_in_dim(kw_pad, t0, WINDOW + NSA_QBLK, axis=1)
        vw = lax.dynamic_slice_in_dim(vw_pad, t0, WINDOW + NSA_QBLK, axis=1)
        q_pos = t0 + jnp.arange(NSA_QBLK, dtype=jnp.int32)
        kw_pos = t0 - WINDOW + jnp.arange(WINDOW + NSA_QBLK, dtype=jnp.int32)
        return nsa_attend(qb, q_pos, gb, kc, vc, ks_blk, vs_blk, kw, vw, kw_pos, rel_bias)

    out = lax.map(block, jnp.arange(T // NSA_QBLK, dtype=jnp.int32))
    return out.transpose(1, 0, 2, 3).reshape(B, T, NSA_HEADS * HEAD_DIM)


def sb_attend(q, q_pos, k, v, k_pos):
    z = jnp.einsum('bqhd,bkhd->bhqk', q, k, preferred_element_type=jnp.float32) * (HEAD_DIM ** -0.5)
    causal = k_pos[None, :] < q_pos[:, None]
    log_1m = jnp.where(causal, -jax.nn.softplus(z), 0.0)
    surv = lax.cumsum(log_1m, axis=3, reverse=True) - log_1m
    a = jnp.where(causal, jnp.exp(jax.nn.log_sigmoid(z) + surv), 0.0)
    return jnp.einsum('bhqk,bkhd->bqhd', a.astype(v.dtype), v)


def sb_prompt(q, k, v):
    B, T = q.shape[:2]
    k_pos = jnp.arange(T, dtype=jnp.int32)

    def block(i):
        t0 = i * SB_QBLK
        qb = lax.dynamic_slice_in_dim(q, t0, SB_QBLK, axis=1)
        return sb_attend(qb, t0 + jnp.arange(SB_QBLK, dtype=jnp.int32), k, v, k_pos)

    out = lax.map(block, jnp.arange(T // SB_QBLK, dtype=jnp.int32))
    return out.transpose(1, 0, 2, 3, 4).reshape(B, T, SB_HEADS * HEAD_DIM)


def mem_attend(q, mk, mv):
    s = jnp.einsum('bqhd,bmhd->bhqm', q, mk, preferred_element_type=jnp.float32) * (MEM_HD ** -0.5)
    p = jax.nn.softmax(s, axis=-1)
    return jnp.einsum('bhqm,bmhd->bqhd', p.astype(mv.dtype), mv)


def merge_out(o_n, o_s, o_m, g_m, p):
    y = (g_m[:, :, 0] * (o_n @ p['w_br_nsa'])
         + g_m[:, :, 1] * (o_s @ p['w_br_sb'])
         + g_m[:, :, 2] * (o_m @ p['w_br_mem']))
    return y @ p['w_out']


def moe(h, w_router, b_router, w_up, b_up, w_down, b_down):
    lead = h.shape[:-1]
    xf = h.reshape(-1, D_MODEL)
    N = xf.shape[0]
    logits = jnp.einsum('nd,de->ne', xf, w_router, preferred_element_type=jnp.float32) + b_router.astype(jnp.float32)
    top_l, top_e = lax.top_k(logits, TOP_K)
    top_w = jax.nn.softmax(top_l, axis=-1)
    n_assign = N * TOP_K
    blk = int(min(MOE_BLOCK, max(8, n_assign // N_EXPERTS)))
    n_rows = (n_assign // blk + N_EXPERTS) * blk
    n_blocks = n_rows // blk
    flat_e = top_e.reshape(-1)
    flat_t = jnp.arange(n_assign, dtype=jnp.int32) // TOP_K
    flat_w = top_w.reshape(-1)
    order = jnp.argsort(flat_e)
    se = flat_e[order]
    counts = jnp.bincount(flat_e, length=N_EXPERTS)
    start = jnp.cumsum(counts) - counts
    padded = (counts + blk - 1) // blk * blk
    p_end = jnp.cumsum(padded)
    p_start = p_end - padded
    dest = p_start[se] + jnp.arange(n_assign) - start[se]
    row_tok = jnp.full((n_rows,), N, jnp.int32).at[dest].set(flat_t[order])
    row_w = jnp.zeros((n_rows,), jnp.float32).at[dest].set(flat_w[order])
    blk_e = jnp.minimum(jnp.searchsorted(p_end, jnp.arange(n_blocks) * blk, side='right'), N_EXPERTS - 1)
    x_pad = jnp.concatenate([xf, jnp.zeros((1, D_MODEL), xf.dtype)], axis=0)

    def run(args):
        e, toks = args
        gu = x_pad[toks] @ w_up[e] + b_up[e]
        gate = jnp.minimum(gu[:, :D_FF], SWIGLU_LIMIT)
        up = jnp.clip(gu[:, D_FF:], -SWIGLU_LIMIT, SWIGLU_LIMIT)
        act = (up + 1.0) * gate * jax.nn.sigmoid(SWIGLU_ALPHA * gate)
        return act @ w_down[e] + b_down[e]

    out = lax.map(run, (blk_e, row_tok.reshape(n_blocks, blk)))
    out = out.reshape(n_rows, D_MODEL) * row_w[:, None].astype(out.dtype)
    y = jnp.zeros((N + 1, D_MODEL), out.dtype).at[row_tok].add(out)[:N]
    return y.reshape(*lead, D_MODEL)


def ffn_sublayer(x, p):
    return x + moe(rmsnorm(x, p['g_ffn']), p['w_router'], p['b_router'],
                   p['w_up'], p['b_up'], p['w_down'], p['b_down'])


def prompt_layer(x, mem, rel_bias, p):
    B, T = x.shape[:2]
    q_n, kv_n, g_n, qkv_s, q_m, g_m = in_proj(x, p['g_attn'], p['w_in_segs'])
    kc = compress(kv_n[:, :, 0], p['cmp_pe_k'], p['cmp_w1_k'], p['cmp_w2_k'])
    vc = compress(kv_n[:, :, 1], p['cmp_pe_v'], p['cmp_w1_v'], p['cmp_w2_v'])
    o_n = nsa_prompt(q_n, g_n, kc, vc, to_blocks(kv_n[:, :, 2]), to_blocks(kv_n[:, :, 3]),
                     kv_n[:, :, 4], kv_n[:, :, 5], rel_bias)
    o_s = sb_prompt(qkv_s[:, :, 0], qkv_s[:, :, 1], qkv_s[:, :, 2])
    mem_kv = (rmsnorm(mem, p['g_mem']) @ p['w_mem_kv']).reshape(B, mem.shape[1], 2, MEM_HEADS, MEM_HD)
    o_m = mem_attend(q_m, mem_kv[:, :, 0], mem_kv[:, :, 1]).reshape(B, T, MEM_HEADS * MEM_HD)
    x = x + merge_out(o_n, o_s, o_m, g_m, p)
    x = ffn_sublayer(x, p)
    wb = min(WINDOW, T)
    return x, (kv_n[:, :, 0:2], kv_n[:, :, 2:4], qkv_s[:, :, 1:3], kv_n[:, T - wb:, 4:6], mem_kv)


def sample_layer(x, c_cmp, c_slc, c_sb, c_win, c_mem, page_table, rel_bias, p):
    B, S = x.shape[:2]
    past = page_table.shape[1] * PAGE_SIZE
    q_n, kv_n, g_n, qkv_s, q_m, g_m = in_proj(x, p['g_attn'], p['w_in_segs'])
    wb = c_win.shape[1]
    win_buf = jnp.concatenate([c_win, kv_n[:, :, 4:6]], axis=1)
    q_pos = past + jnp.arange(S, dtype=jnp.int32)
    kw_pos = past - wb + jnp.arange(wb + S, dtype=jnp.int32)
    k_pos = jnp.arange(past + S, dtype=jnp.int32)

    def per_seq(args):
        pt, qn, gn, qs, new_n, new_s, wbuf = args
        cmp_rows = jnp.concatenate([c_cmp[pt].reshape(past, 2, NSA_KV, HEAD_DIM), new_n[:, 0:2]], axis=0)[None]
        slc_rows = jnp.concatenate([c_slc[pt].reshape(past, 2, NSA_KV, HEAD_DIM), new_n[:, 2:4]], axis=0)[None]
        sb_rows = jnp.concatenate([c_sb[pt].reshape(past, 2, SB_HEADS, HEAD_DIM), new_s], axis=0)[None]
        kc = compress(cmp_rows[:, :, 0], p['cmp_pe_k'], p['cmp_w1_k'], p['cmp_w2_k'])
        vc = compress(cmp_rows[:, :, 1], p['cmp_pe_v'], p['cmp_w1_v'], p['cmp_w2_v'])
        o_n = nsa_attend(qn[None], q_pos, gn[None], kc, vc, to_blocks(slc_rows[:, :, 0]),
                         to_blocks(slc_rows[:, :, 1]), wbuf[None, :, 0], wbuf[None, :, 1], kw_pos, rel_bias)
        o_s = sb_attend(qs[None], q_pos, sb_rows[:, :, 0], sb_rows[:, :, 1], k_pos)
        return o_n[0], o_s[0].reshape(S, SB_HEADS * HEAD_DIM)

    o_n, o_s = lax.map(per_seq, (page_table, q_n, g_n, qkv_s[:, :, 0], kv_n, qkv_s[:, :, 1:3], win_buf))
    o_m = mem_attend(q_m, c_mem[:, :, 0], c_mem[:, :, 1]).reshape(B, S, MEM_HEADS * MEM_HD)
    x = x + merge_out(o_n, o_s, o_m, g_m, p)
    x = ffn_sublayer(x, p)
    return x, (kv_n[:, :, 0:2], kv_n[:, :, 2:4], qkv_s[:, :, 1:3], win_buf[:, S:])


def kernel(x_prompt, x_sample, mem_prompt, cache_nsa_cmp, cache_nsa_slc, cache_sb, cache_nsa_win,
           cache_mem, page_table, rel_bias, g_attn, w_in, cmp_pe_k, cmp_w1_k, cmp_w2_k, cmp_pe_v,
           cmp_w1_v, cmp_w2_v, g_mem, w_mem_kv, w_br_nsa, w_br_sb, w_br_mem, w_out, g_ffn, w_router,
           b_router, w_up, b_up, w_down, b_down, g_final):
    depth = g_attn.shape[0]
    xp, xs = x_prompt, x_sample
    st_p, st_s = [], []
    for l in range(depth):
        p = {'g_attn': g_attn[l], 'w_in_segs': split_w_in(w_in[l]),
             'cmp_pe_k': cmp_pe_k[l], 'cmp_w1_k': cmp_w1_k[l], 'cmp_w2_k': cmp_w2_k[l],
             'cmp_pe_v': cmp_pe_v[l], 'cmp_w1_v': cmp_w1_v[l], 'cmp_w2_v': cmp_w2_v[l],
             'g_mem': g_mem[l], 'w_mem_kv': w_mem_kv[l],
             'w_br_nsa': w_br_nsa[l], 'w_br_sb': w_br_sb[l], 'w_br_mem': w_br_mem[l], 'w_out': w_out[l],
             'g_ffn': g_ffn[l], 'w_router': w_router[l], 'b_router': b_router[l],
             'w_up': w_up[l], 'b_up': b_up[l], 'w_down': w_down[l], 'b_down': b_down[l]}
        xp, sp = prompt_layer(xp, mem_prompt, rel_bias, p)
        xs, ss = sample_layer(xs, cache_nsa_cmp[l], cache_nsa_slc[l], cache_sb[l], cache_nsa_win[l],
                              cache_mem[l], page_table, rel_bias, p)
        st_p.append(sp)
        st_s.append(ss)
    y_prompt = rmsnorm(xp, g_final)
    y_sample = rmsnorm(xs, g_final)

    def stacked(states, i):
        return jnp.stack([s[i] for s in states])

    return (y_prompt, y_sample,
            stacked(st_p, 0), stacked(st_p, 1), stacked(st_p, 2), stacked(st_p, 3), stacked(st_p, 4),
            stacked(st_s, 0), stacked(st_s, 1), stacked(st_s, 2), stacked(st_s, 3))
```

```python
import functools
import math

import jax
import jax.numpy as jnp
import numpy as np
from jax import lax
from jax.experimental import pallas as pl
from jax.experimental.pallas import tpu as pltpu

D_MODEL = 1024
PAGE_SIZE = 128
NSA_HEADS = 8
NSA_KV = 2
NSA_HPG = NSA_HEADS // NSA_KV
HEAD_DIM = 64
CMP_LEN = 32
CMP_STRIDE = 16
CMP_HID = 128
SLC_LEN = 64
SLC_TOP = 16
WINDOW = 512
NSA_QBLK = 32
FORCE_SCORE = 1.0e6
SB_HEADS = 8
SB_QBLK = 128
MEM_HEADS = 4
MEM_HD = 128
REL_BUCKETS = 32
REL_MAX_DIST = 2048
N_EXPERTS = 32
TOP_K = 4
D_FF = 1024
SWIGLU_LIMIT = 7.0
SWIGLU_ALPHA = 1.702
MOE_BLOCK = 256
EPS = 1e-6
NEG_INF = -1.0e30

SEG_QN = NSA_HEADS * HEAD_DIM
SEG_KVN = 6 * NSA_KV * HEAD_DIM
SEG_GN = 3 * NSA_HEADS
SEG_QKVS = 3 * SB_HEADS * HEAD_DIM
SEG_QM = MEM_HEADS * MEM_HD
SEG_GM = 3 * D_MODEL
IN_SIZES = (SEG_QN, SEG_KVN, SEG_GN, SEG_QKVS, SEG_QM, SEG_GM)

VMEM_LIMIT = 48 * 1024 * 1024


def _rms_proj_kernel(x_ref, g_ref, *refs, n_seg, sigmoid):
    w_refs = refs[:n_seg]
    o_refs = refs[n_seg:]
    x = x_ref[...]
    h = x * lax.rsqrt(jnp.mean(x * x, axis=-1, keepdims=True) + EPS) * g_ref[...]
    hb = h.astype(jnp.bfloat16)
    for w_ref, o_ref, sig in zip(w_refs, o_refs, sigmoid):
        z = jnp.dot(hb, w_ref[...], preferred_element_type=jnp.float32)
        if sig:
            z = jax.nn.sigmoid(z)
        o_ref[...] = z.astype(o_ref.dtype)


def rms_proj(x2d, g, weights, sigmoid, out_dtypes, tm):
    n, d = x2d.shape
    assert n % tm == 0
    n_seg = len(weights)
    in_specs = [pl.BlockSpec((tm, d), lambda i: (i, 0)),
                pl.BlockSpec((1, d), lambda i: (0, 0))]
    for w in weights:
        in_specs.append(pl.BlockSpec(w.shape, lambda i: (0, 0), pipeline_mode=pl.Buffered(1)))
    out_specs = [pl.BlockSpec((tm, w.shape[1]), lambda i: (i, 0)) for w in weights]
    out_shape = [jax.ShapeDtypeStruct((n, w.shape[1]), dt) for w, dt in zip(weights, out_dtypes)]
    return pl.pallas_call(
        functools.partial(_rms_proj_kernel, n_seg=n_seg, sigmoid=tuple(sigmoid)),
        grid=(n // tm,),
        in_specs=in_specs,
        out_specs=out_specs,
        out_shape=out_shape,
        compiler_params=pltpu.CompilerParams(dimension_semantics=("parallel",),
                                             vmem_limit_bytes=VMEM_LIMIT),
        name="rms_proj",
    )(x2d, g.reshape(1, d), *weights)


def in_proj(x, g_attn, w_segs):
    B, T, D = x.shape
    n = B * T
    tm = 256 if n % 256 == 0 else n
    outs = rms_proj(x.reshape(n, D), g_attn, w_segs,
                    sigmoid=(False, False, True, False, False, True),
                    out_dtypes=(jnp.float32,) * 6, tm=tm)
    q_n, kv_n, g_n, qkv_s, q_m, g_m = outs
    return (q_n.reshape(B, T, NSA_KV, NSA_HPG, HEAD_DIM),
            kv_n.reshape(B, T, 6, NSA_KV, HEAD_DIM),
            g_n.reshape(B, T, NSA_KV, NSA_HPG, 3),
            qkv_s.reshape(B, T, 3, SB_HEADS, HEAD_DIM),
            q_m.reshape(B, T, MEM_HEADS, MEM_HD),
            g_m.reshape(B, T, 3, D_MODEL))


def split_w_in(w_in):
    offs = np.cumsum((0,) + IN_SIZES)
    return [w_in[:, int(a):int(b)].astype(jnp.bfloat16) for a, b in zip(offs[:-1], offs[1:])]


def _nt_dot(a, b):
    return lax.dot_general(a, b, (((1,), (1,)), ((), ())), preferred_element_type=jnp.float32)


def _online_step(s, mask, v_aug, m, acc, cdt):
    h, tq, tk = s.shape
    s = jnp.where(mask[None], s, NEG_INF)
    m_new = jnp.maximum(m, jnp.max(s, axis=-1, keepdims=True))
    p = jnp.where(mask[None], jnp.exp(s - m_new), 0.0)
    alpha = jnp.exp(m - m_new).reshape(h * tq, 1)
    pv = jnp.dot(p.reshape(h * tq, tk).astype(cdt), v_aug, preferred_element_type=jnp.float32)
    return m_new, alpha * acc + pv


def _finish(acc):
    l = acc[:, HEAD_DIM:HEAD_DIM + 1]
    return acc[:, :HEAD_DIM] / jnp.where(l > 0.0, l, 1.0)


def _nsa_prompt_kernel(q_ref, gate_ref, kc_ref, vc_ref, ks_ref, vs_ref, kw_ref, vw_ref, cb_ref, tb_ref,
                       o_ref, *, tq, n_cmp, ns, cdt):
    f32 = jnp.float32
    qi = pl.program_id(2)
    q0 = qi * tq
    tk = tq
    hpg = NSA_HPG
    rows = hpg * tq
    q = q_ref[0].reshape(rows, HEAD_DIM)

    kc = kc_ref[0, 0]
    ncp = kc.shape[0]
    qpos_c = q0 + lax.broadcasted_iota(jnp.int32, (tq, ncp), 0)
    cidx = lax.broadcasted_iota(jnp.int32, (tq, ncp), 1)
    cmask = (cidx * CMP_STRIDE + (CMP_LEN - 1) <= qpos_c) & (cidx < n_cmp)
    s_c = (_nt_dot(q, kc) + cb_ref[0, 0]).reshape(hpg, tq, ncp)
    s_c = jnp.where(cmask[None], s_c, NEG_INF)
    m_c = jnp.max(s_c, axis=-1, keepdims=True)
    p_c = jnp.where(cmask[None], jnp.exp(s_c - m_c), 0.0)
    l_c = jnp.sum(p_c, axis=-1, keepdims=True)
    p_c = p_c / jnp.where(l_c > 0.0, l_c, 1.0)
    o_c = jnp.dot(p_c.reshape(rows, ncp).astype(cdt), vc_ref[0, 0], preferred_element_type=f32)

    p_sum = p_c[0]
    for h in range(1, hpg):
        p_sum = p_sum + p_c[h]
    jb = lax.broadcasted_iota(jnp.int32, (ns, ncp), 0)
    cc = lax.broadcasted_iota(jnp.int32, (ns, ncp), 1)
    ov_t = ((cc * CMP_STRIDE < jb * SLC_LEN + SLC_LEN)
            & (cc * CMP_STRIDE + (CMP_LEN - 1) >= jb * SLC_LEN)).astype(cdt)
    p_hi = p_sum.astype(cdt)
    p_lo = (p_sum - p_hi.astype(f32)).astype(cdt)
    imp = _nt_dot(ov_t, p_hi) + _nt_dot(ov_t, p_lo)
    blk = lax.broadcasted_iota(jnp.int32, (ns, tq), 0)
    qp_t = q0 + lax.broadcasted_iota(jnp.int32, (ns, tq), 1)
    cur = qp_t // SLC_LEN
    forced = (blk == 0) | (blk == cur) | (blk == cur - 1)
    imp = jnp.where(forced, FORCE_SCORE, imp)
    imp = jnp.where(blk * SLC_LEN <= qp_t, imp, -1.0)
    cnt = jnp.zeros((ns, tq), f32)
    for i in range(ns):
        row = imp[i:i + 1, :]
        ahead = (row > imp) | ((row == imp) & (blk > i))
        cnt = cnt + jnp.where(ahead, 1.0, 0.0)
    sel_t = jnp.where((cnt < float(min(SLC_TOP, ns))) & (imp >= 0.0), 1.0, 0.0)
    sel = sel_t.T.astype(cdt)

    qpos = q0 + lax.broadcasted_iota(jnp.int32, (tq, tk), 0)
    kcol = lax.broadcasted_iota(jnp.int32, (tq, tk), 1)
    eb = lax.broadcasted_iota(jnp.int32, (ns, tk), 0)
    ek = lax.broadcasted_iota(jnp.int32, (ns, tk), 1)
    m0 = jnp.full((hpg, tq, 1), NEG_INF, f32)
    a0 = jnp.zeros((rows, 2 * HEAD_DIM), f32)

    def sel_body(kb, carry):
        m, acc = carry
        k0 = pl.multiple_of(kb * tk, tk)
        s = (_nt_dot(q, ks_ref[0, 0, pl.ds(k0, tk), :]) + tb_ref[0, qi - kb]).reshape(hpg, tq, tk)
        expand = jnp.where(eb == (k0 + ek) // SLC_LEN, 1.0, 0.0).astype(cdt)
        picked = jnp.dot(sel, expand, preferred_element_type=f32)
        mask = (picked > 0.5) & (k0 + kcol <= qpos)
        return _online_step(s, mask, vs_ref[0, 0, pl.ds(k0, tk), :], m, acc, cdt)

    _, acc_s = lax.fori_loop(0, qi + 1, sel_body, (m0, a0))
    o_s = _finish(acc_s)

    def win_body(kb, carry):
        m, acc = carry
        k0 = pl.multiple_of(kb * tk, tk)
        s = (_nt_dot(q, kw_ref[0, 0, pl.ds(k0, tk), :]) + tb_ref[0, qi - kb]).reshape(hpg, tq, tk)
        dist = qpos - (k0 + kcol)
        mask = (dist >= 0) & (dist < WINDOW)
        return _online_step(s, mask, vw_ref[0, 0, pl.ds(k0, tk), :], m, acc, cdt)

    _, acc_w = lax.fori_loop(jnp.maximum(qi - WINDOW // tk, 0), qi + 1, win_body, (m0, a0))
    o_w = _finish(acc_w)

    gates = gate_ref[0, 0]
    outs = []
    for h in range(hpg):
        r = slice(h * tq, (h + 1) * tq)
        outs.append(gates[:, 3 * h:3 * h + 1] * o_c[r]
                    + gates[:, 3 * h + 1:3 * h + 2] * o_s[r]
                    + gates[:, 3 * h + 2:3 * h + 3] * o_w[r])
    o_ref[0] = jnp.concatenate(outs, axis=-1).astype(o_ref.dtype)


def _aug_ones(v):
    return jnp.concatenate([v, jnp.ones_like(v)], axis=-1)


def nsa_bias_tables(rel_bias, t_len, tq):
    nq = t_len // tq
    ncp = t_len // CMP_STRIDE
    bias_d = rel_bias[rel_bucket(jnp.arange(t_len, dtype=jnp.int32))]
    bias_d = bias_d.reshape(t_len, NSA_KV, NSA_HPG).transpose(1, 2, 0)
    i = jnp.arange(tq)[:, None]
    d_t = jnp.arange(nq)[:, None, None] * tq + i[None] - jnp.arange(tq)[None, None, :]
    tb = bias_d[:, :, jnp.clip(d_t, 0, t_len - 1)]
    tb = tb.transpose(0, 2, 1, 3, 4).reshape(NSA_KV, nq, NSA_HPG * tq, tq)
    c_end = jnp.arange(ncp) * CMP_STRIDE + (CMP_LEN - 1)
    d_c = jnp.arange(nq)[:, None, None] * tq + i[None] - c_end[None, None, :]
    cb = bias_d[:, :, jnp.clip(d_c, 0, t_len - 1)]
    cb = cb.transpose(0, 2, 1, 3, 4).reshape(NSA_KV, nq, NSA_HPG * tq, ncp)
    return tb, cb


def nsa_prompt_pallas(q, gates, kc, vc, k_slc, v_slc, k_win, v_win, rel_bias, *, tq=128, cdt=jnp.bfloat16):
    B, T = q.shape[:2]
    n_cmp = kc.shape[1]
    ncp = T // CMP_STRIDE
    ns = T // SLC_LEN
    nq = T // tq
    rows = NSA_HPG * tq
    tb, cb = nsa_bias_tables(rel_bias, T, tq)
    to_g = lambda a: a.astype(cdt).transpose(0, 2, 1, 3)
    qh = q.astype(cdt).reshape(B, T, NSA_HEADS, HEAD_DIM).transpose(0, 2, 1, 3)
    gt = gates.astype(jnp.float32).reshape(B, T, NSA_KV, NSA_HPG * 3).transpose(0, 2, 1, 3)
    pad_c = ((0, 0), (0, ncp - n_cmp), (0, 0), (0, 0))
    kcg = to_g(jnp.pad(kc, pad_c))
    vcg = to_g(jnp.pad(vc, pad_c))
    ksg, kwg = to_g(k_slc), to_g(k_win)
    vsg, vwg = _aug_ones(to_g(v_slc)), _aug_ones(to_g(v_win))
    per_bg = lambda w: pl.BlockSpec((1, 1, T, w), lambda g, b, i: (b, g, 0, 0))
    per_bg_c = lambda w: pl.BlockSpec((1, 1, ncp, w), lambda g, b, i: (b, g, 0, 0))
    return pl.pallas_call(
        functools.partial(_nsa_prompt_kernel, tq=tq, n_cmp=n_cmp, ns=ns, cdt=cdt),
        grid=(NSA_KV, B, nq),
        in_specs=[
            pl.BlockSpec((1, NSA_HPG, tq, HEAD_DIM), lambda g, b, i: (b, g, i, 0)),
            pl.BlockSpec((1, 1, tq, NSA_HPG * 3), lambda g, b, i: (b, g, i, 0)),
            per_bg_c(HEAD_DIM), per_bg_c(HEAD_DIM),
            per_bg(HEAD_DIM), per_bg(2 * HEAD_DIM), per_bg(HEAD_DIM), per_bg(2 * HEAD_DIM),
            pl.BlockSpec((1, 1, rows, ncp), lambda g, b, i: (g, i, 0, 0)),
            pl.BlockSpec((1, nq, rows, tq), lambda g, b, i: (g, 0, 0, 0), pipeline_mode=pl.Buffered(1)),
        ],
        out_specs=pl.BlockSpec((1, tq, NSA_HPG * HEAD_DIM), lambda g, b, i: (b, i, g)),
        out_shape=jax.ShapeDtypeStruct((B, T, NSA_HEADS * HEAD_DIM), cdt),
        compiler_params=pltpu.CompilerParams(dimension_semantics=("parallel", "parallel", "parallel"),
                                             vmem_limit_bytes=VMEM_LIMIT),
        name="nsa_prompt",
    )(qh, gt, kcg, vcg, ksg, vsg, kwg, vwg, cb, tb)


SB_DEAD_LOG = -104.0


def _split3(x, cdt):
    hi = x.astype(cdt)
    r = x - hi.astype(jnp.float32)
    mid = r.astype(cdt)
    lo = (r - mid.astype(jnp.float32)).astype(cdt)
    return hi, mid, lo


def _sb_tile(z, causal, carry, v, tri, cdt):
    h, tq, tk = z.shape
    sp = jnp.maximum(z, 0.0) + jnp.log1p(jnp.exp(-jnp.abs(z)))
    l1m = jnp.where(causal[None], -sp, 0.0).reshape(h * tq, tk)
    suffix = None
    for part in _split3(l1m, cdt):
        d = jnp.dot(part, tri, preferred_element_type=jnp.float32)
        suffix = d if suffix is None else suffix + d
    surv = (suffix + carry).reshape(h, tq, tk)
    a = jnp.where(causal[None], jnp.exp(z - sp + surv), 0.0)
    pv = jnp.einsum('hqk,hkd->hqd', a.astype(cdt), v, preferred_element_type=jnp.float32)
    return pv, carry + suffix[:, 0:1] + l1m[:, 0:1]


def _sb_prompt_kernel(q_ref, k_ref, v_ref, o_ref, *, tq, hs, cdt):
    f32 = jnp.float32
    qi = pl.program_id(2)
    q0 = qi * tq
    tk = tq
    q = q_ref[0, 0]
    qpos = q0 + lax.broadcasted_iota(jnp.int32, (tq, tk), 0)
    kcol = lax.broadcasted_iota(jnp.int32, (tq, tk), 1)
    tri = jnp.where(lax.broadcasted_iota(jnp.int32, (tk, tk), 0) > lax.broadcasted_iota(jnp.int32, (tk, tk), 1),
                    1.0, 0.0).astype(cdt)

    def cond(state):
        kb, live, _, _ = state
        return (kb >= 0) & (live > SB_DEAD_LOG)

    def body(state):
        kb, _, carry, acc = state
        k0 = pl.multiple_of(kb * tk, tk)
        k = k_ref[0, 0, :, pl.ds(k0, tk), :]
        v = v_ref[0, 0, :, pl.ds(k0, tk), :]
        z = jnp.einsum('hqd,hkd->hqk', q, k, preferred_element_type=f32)
        pv, carry = _sb_tile(z, k0 + kcol < qpos, carry, v, tri, cdt)
        return kb - 1, jnp.max(carry), carry, acc + pv

    init = (qi, jnp.float32(0.0), jnp.zeros((hs * tq, 1), f32), jnp.zeros((hs, tq, HEAD_DIM), f32))
    _, _, _, acc = lax.while_loop(cond, body, init)
    o_ref[0] = jnp.concatenate([acc[h] for h in range(hs)], axis=-1).astype(o_ref.dtype)


def sb_prompt_pallas(qkv, *, tq=128, hs=4, cdt=jnp.bfloat16):
    B, T = qkv.shape[:2]
    x = qkv.astype(cdt).transpose(0, 2, 3, 1, 4)
    full = lambda j: pl.BlockSpec((1, 1, hs, T, HEAD_DIM), lambda b, g, i: (b, j, g, 0, 0))
    return pl.pallas_call(
        functools.partial(_sb_prompt_kernel, tq=tq, hs=hs, cdt=cdt),
        grid=(B, SB_HEADS // hs, T // tq),
        in_specs=[pl.BlockSpec((1, 1, hs, tq, HEAD_DIM), lambda b, g, i: (b, 0, g, i, 0)), full(1), full(2)],
        out_specs=pl.BlockSpec((1, tq, hs * HEAD_DIM), lambda b, g, i: (b, i, g)),
        out_shape=jax.ShapeDtypeStruct((B, T, SB_HEADS * HEAD_DIM), cdt),
        compiler_params=pltpu.CompilerParams(dimension_semantics=("parallel", "parallel", "parallel"),
                                             vmem_limit_bytes=VMEM_LIMIT),
        name="sb_prompt",
    )(x, x, x)


def rmsnorm(x, g):
    xf = x.astype(jnp.float32)
    xf = xf * lax.rsqrt(jnp.mean(xf * xf, axis=-1, keepdims=True) + EPS)
    return (xf * g.astype(jnp.float32)).astype(x.dtype)


def masked_softmax(s, mask):
    p = jax.nn.softmax(jnp.where(mask, s, NEG_INF), axis=-1)
    return jnp.where(mask, p, 0.0)


def rel_bucket(dist):
    n = jnp.maximum(dist, 0)
    max_exact = REL_BUCKETS // 2
    nf = jnp.maximum(n, 1).astype(jnp.float32)
    large = max_exact + (jnp.log(nf / max_exact) / math.log(REL_MAX_DIST / max_exact)
                         * (REL_BUCKETS - max_exact)).astype(jnp.int32)
    return jnp.where(n < max_exact, n, jnp.minimum(large, REL_BUCKETS - 1))


def compress(x, pe, w1, w2):
    B, T = x.shape[:2]
    R = CMP_LEN // CMP_STRIDE
    nh = T // CMP_STRIDE
    nc = nh - R + 1
    halves = x[:, :nh * CMP_STRIDE].reshape(B, nh, CMP_STRIDE, NSA_KV, HEAD_DIM)
    w1h = w1.reshape(R, CMP_STRIDE, HEAD_DIM, CMP_HID)
    pre = pe.reshape(-1) @ w1
    for r in range(R):
        pre = pre + jnp.einsum('bnsgd,sdh->bngh', halves[:, r:r + nc], w1h[r])
    return jax.nn.gelu(pre) @ w2


def to_blocks(x):
    B, T = x.shape[:2]
    ns = -(-T // SLC_LEN)
    x = jnp.pad(x, ((0, 0), (0, ns * SLC_LEN - T), (0, 0), (0, 0)))
    return x.reshape(B, ns, SLC_LEN, NSA_KV, HEAD_DIM).transpose(0, 3, 1, 2, 4)


def nsa_attend(q, q_pos, gates, kc, vc, ks_blk, vs_blk, kw, vw, kw_pos, rel_bias):
    B, Q = q.shape[:2]
    NC = kc.shape[1]
    NS = ks_blk.shape[2]
    KW = kw.shape[1]
    scale = HEAD_DIM ** -0.5
    f32 = jnp.float32
    c_start = jnp.arange(NC, dtype=jnp.int32) * CMP_STRIDE
    c_end = c_start + (CMP_LEN - 1)
    c_mask = (c_end[None, :] <= q_pos[:, None])[:, None, None, :]
    c_bias = rel_bias[rel_bucket(q_pos[:, None] - c_end[None, :])].reshape(
        Q, NC, NSA_KV, NSA_HPG).transpose(0, 2, 3, 1)
    s_c = jnp.einsum('bqghd,bcgd->bqghc', q, kc, preferred_element_type=f32) * scale + c_bias
    p_c = masked_softmax(s_c, c_mask)
    o_c = jnp.einsum('bqghc,bcgd->bqghd', p_c.astype(vc.dtype), vc)
    s_start = jnp.arange(NS, dtype=jnp.int32) * SLC_LEN
    overlap = ((c_start[:, None] < s_start[None, :] + SLC_LEN)
               & (c_end[:, None] >= s_start[None, :])).astype(f32)
    imp = jnp.einsum('bqghc,cj->bqgj', p_c, overlap)
    blk = jnp.arange(NS, dtype=jnp.int32)[None, :]
    cur = (q_pos // SLC_LEN)[:, None]
    forced = (blk == 0) | (blk == cur) | (blk == cur - 1)
    valid = s_start[None, :] <= q_pos[:, None]
    imp = jnp.where(forced[None, :, None, :], FORCE_SCORE, imp)
    imp = jnp.where(valid[None, :, None, :], imp, -1.0)
    n_top = min(SLC_TOP, NS)
    top_val, idx = lax.top_k(imp, n_top)
    bi = jnp.arange(B)[:, None, None, None]
    gi = jnp.arange(NSA_KV)[None, None, :, None]
    k_sel = ks_blk[bi, gi, idx]
    v_sel = vs_blk[bi, gi, idx]
    k_pos = idx[..., None] * SLC_LEN + jnp.arange(SLC_LEN, dtype=jnp.int32)
    qp = q_pos[None, :, None, None, None]
    s_mask = (top_val >= 0)[..., None] & (k_pos <= qp)
    bias_g = rel_bias.reshape(REL_BUCKETS, NSA_KV, NSA_HPG).transpose(1, 0, 2)
    s_bias = bias_g[gi[..., None], rel_bucket(qp - k_pos)]
    n_keys = n_top * SLC_LEN
    s_bias = s_bias.transpose(0, 1, 2, 5, 3, 4).reshape(B, Q, NSA_KV, NSA_HPG, n_keys)
    s_s = jnp.einsum('bqghd,bqgnkd->bqghnk', q, k_sel, preferred_element_type=f32).reshape(
        B, Q, NSA_KV, NSA_HPG, n_keys) * scale + s_bias
    p_s = masked_softmax(s_s, s_mask.reshape(B, Q, NSA_KV, 1, n_keys))
    o_s = jnp.einsum('bqghk,bqgkd->bqghd', p_s.astype(v_sel.dtype),
                     v_sel.reshape(B, Q, NSA_KV, n_keys, HEAD_DIM))
    w_dist = q_pos[:, None] - kw_pos[None, :]
    w_mask = ((w_dist >= 0) & (w_dist < WINDOW) & (kw_pos[None, :] >= 0))[:, None, None, :]
    w_bias = rel_bias[rel_bucket(w_dist)].reshape(Q, KW, NSA_KV, NSA_HPG).transpose(0, 2, 3, 1)
    s_w = jnp.einsum('bqghd,bkgd->bqghk', q, kw, preferred_element_type=f32) * scale + w_bias
    p_w = masked_softmax(s_w, w_mask)
    o_w = jnp.einsum('bqghk,bkgd->bqghd', p_w.astype(vw.dtype), vw)
    g = gates.astype(o_c.dtype)
    o = g[..., 0:1] * o_c + g[..., 1:2] * o_s + g[..., 2:3] * o_w
    return o.reshape(B, Q, NSA_HEADS * HEAD_DIM)


def nsa_prompt(q, gates, kc, vc, ks_blk, vs_blk, k_win, v_win, rel_bias):
    B, T = q.shape[:2]
    pad = ((0, 0), (WINDOW, 0), (0, 0), (0, 0))
    kw_pad = jnp.pad(k_win, pad)
    vw_pad = jnp.pad(v_win, pad)

    def block(i):
        t0 = i * NSA_QBLK
        qb = lax.dynamic_slice_in_dim(q, t0, NSA_QBLK, axis=1)
        gb = lax.dynamic_slice_in_dim(gates, t0, NSA_QBLK, axis=1)
        kw = lax.dynamic_slice_in_dim(kw_pad, t0, WINDOW + NSA_QBLK, axis=1)
        vw = lax.dynamic_slice_in_dim(vw_pad, t0, WINDOW + NSA_QBLK, axis=1)
        q_pos = t0 + jnp.arange(NSA_QBLK, dtype=jnp.int32)
        kw_pos = t0 - WINDOW + jnp.arange(WINDOW + NSA_QBLK, dtype=jnp.int32)
        return nsa_attend(qb, q_pos, gb, kc, vc, ks_blk, vs_blk, kw, vw, kw_pos, rel_bias)

    out = lax.map(block, jnp.arange(T // NSA_QBLK, dtype=jnp.int32))
    return out.transpose(1, 0, 2, 3).reshape(B, T, NSA_HEADS * HEAD_DIM)


def sb_attend(q, q_pos, k, v, k_pos):
    z = jnp.einsum('bqhd,bkhd->bhqk', q, k, preferred_element_type=jnp.float32) * (HEAD_DIM ** -0.5)
    causal = k_pos[None, :] < q_pos[:, None]
    log_1m = jnp.where(causal, -jax.nn.softplus(z), 0.0)
    surv = lax.cumsum(log_1m, axis=3, reverse=True) - log_1m
    a = jnp.where(causal, jnp.exp(jax.nn.log_sigmoid(z) + surv), 0.0)
    return jnp.einsum('bhqk,bkhd->bqhd', a.astype(v.dtype), v)


def sb_prompt(q, k, v):
    B, T = q.shape[:2]
    k_pos = jnp.arange(T, dtype=jnp.int32)

    def block(i):
        t0 = i * SB_QBLK
        qb = lax.dynamic_slice_in_dim(q, t0, SB_QBLK, axis=1)
        return sb_attend(qb, t0 + jnp.arange(SB_QBLK, dtype=jnp.int32), k, v, k_pos)

    out = lax.map(block, jnp.arange(T // SB_QBLK, dtype=jnp.int32))
    return out.transpose(1, 0, 2, 3, 4).reshape(B, T, SB_HEADS * HEAD_DIM)


def mem_attend(q, mk, mv):
    s = jnp.einsum('bqhd,bmhd->bhqm', q, mk, preferred_element_type=jnp.float32) * (MEM_HD ** -0.5)
    p = jax.nn.softmax(s, axis=-1)
    return jnp.einsum('bhqm,bmhd->bqhd', p.astype(mv.dtype), mv)


def merge_out(o_n, o_s, o_m, g_m, p):
    y = (g_m[:, :, 0] * (o_n @ p['w_br_nsa'])
         + g_m[:, :, 1] * (o_s @ p['w_br_sb'])
         + g_m[:, :, 2] * (o_m @ p['w_br_mem']))
    return y @ p['w_out']


def moe(h, w_router, b_router, w_up, b_up, w_down, b_down):
    lead = h.shape[:-1]
    xf = h.reshape(-1, D_MODEL)
    N = xf.shape[0]
    logits = jnp.einsum('nd,de->ne', xf, w_router, preferred_element_type=jnp.float32) + b_router.astype(jnp.float32)
    top_l, top_e = lax.top_k(logits, TOP_K)
    top_w = jax.nn.softmax(top_l, axis=-1)
    n_assign = N * TOP_K
    blk = int(min(MOE_BLOCK, max(8, n_assign // N_EXPERTS)))
    n_rows = (n_assign // blk + N_EXPERTS) * blk
    n_blocks = n_rows // blk
    flat_e = top_e.reshape(-1)
    flat_t = jnp.arange(n_assign, dtype=jnp.int32) // TOP_K
    flat_w = top_w.reshape(-1)
    order = jnp.argsort(flat_e)
    se = flat_e[order]
    counts = jnp.bincount(flat_e, length=N_EXPERTS)
    start = jnp.cumsum(counts) - counts
    padded = (counts + blk - 1) // blk * blk
    p_end = jnp.cumsum(padded)
    p_start = p_end - padded
    dest = p_start[se] + jnp.arange(n_assign) - start[se]
    row_tok = jnp.full((n_rows,), N, jnp.int32).at[dest].set(flat_t[order])
    row_w = jnp.zeros((n_rows,), jnp.float32).at[dest].set(flat_w[order])
    blk_e = jnp.minimum(jnp.searchsorted(p_end, jnp.arange(n_blocks) * blk, side='right'), N_EXPERTS - 1)
    x_pad = jnp.concatenate([xf, jnp.zeros((1, D_MODEL), xf.dtype)], axis=0)

    def run(args):
        e, toks = args
        gu = x_pad[toks] @ w_up[e] + b_up[e]
        gate = jnp.minimum(gu[:, :D_FF], SWIGLU_LIMIT)
        up = jnp.clip(gu[:, D_FF:], -SWIGLU_LIMIT, SWIGLU_LIMIT)
        act = (up + 1.0) * gate * jax.nn.sigmoid(SWIGLU_ALPHA * gate)
        return act @ w_down[e] + b_down[e]

    out = lax.map(run, (blk_e, row_tok.reshape(n_blocks, blk)))
    out = out.reshape(n_rows, D_MODEL) * row_w[:, None].astype(out.dtype)
    y = jnp.zeros((N + 1, D_MODEL), out.dtype).at[row_tok].add(out)[:N]
    return y.reshape(*lead, D_MODEL)


def ffn_sublayer(x, p):
    return x + moe(rmsnorm(x, p['g_ffn']), p['w_router'], p['b_router'],
                   p['w_up'], p['b_up'], p['w_down'], p['b_down'])


def prompt_layer(x, mem, rel_bias, p):
    B, T = x.shape[:2]
    q_n, kv_n, g_n, qkv_s, q_m, g_m = in_proj(x, p['g_attn'], p['w_in_segs'])
    kc = compress(kv_n[:, :, 0], p['cmp_pe_k'], p['cmp_w1_k'], p['cmp_w2_k'])
    vc = compress(kv_n[:, :, 1], p['cmp_pe_v'], p['cmp_w1_v'], p['cmp_w2_v'])
    att_scale = HEAD_DIM ** -0.5
    o_n = nsa_prompt_pallas(q_n * att_scale, g_n, kc, vc, kv_n[:, :, 2], kv_n[:, :, 3],
                            kv_n[:, :, 4], kv_n[:, :, 5], rel_bias)
    o_s = sb_prompt_pallas(qkv_s * jnp.array([att_scale, 1.0, 1.0], jnp.float32).reshape(1, 1, 3, 1, 1))
    mem_kv = (rmsnorm(mem, p['g_mem']) @ p['w_mem_kv']).reshape(B, mem.shape[1], 2, MEM_HEADS, MEM_HD)
    o_m = mem_attend(q_m, mem_kv[:, :, 0], mem_kv[:, :, 1]).reshape(B, T, MEM_HEADS * MEM_HD)
    x = x + merge_out(o_n, o_s, o_m, g_m, p)
    x = ffn_sublayer(x, p)
    wb = min(WINDOW, T)
    return x, (kv_n[:, :, 0:2], kv_n[:, :, 2:4], qkv_s[:, :, 1:3], kv_n[:, T - wb:, 4:6], mem_kv)


def sample_layer(x, c_cmp, c_slc, c_sb, c_win, c_mem, page_table, rel_bias, p):
    B, S = x.shape[:2]
    past = page_table.shape[1] * PAGE_SIZE
    q_n, kv_n, g_n, qkv_s, q_m, g_m = in_proj(x, p['g_attn'], p['w_in_segs'])
    wb = c_win.shape[1]
    win_buf = jnp.concatenate([c_win, kv_n[:, :, 4:6]], axis=1)
    q_pos = past + jnp.arange(S, dtype=jnp.int32)
    kw_pos = past - wb + jnp.arange(wb + S, dtype=jnp.int32)
    k_pos = jnp.arange(past + S, dtype=jnp.int32)

    def per_seq(args):
        pt, qn, gn, qs, new_n, new_s, wbuf = args
        cmp_rows = jnp.concatenate([c_cmp[pt].reshape(past, 2, NSA_KV, HEAD_DIM), new_n[:, 0:2]], axis=0)[None]
        slc_rows = jnp.concatenate([c_slc[pt].reshape(past, 2, NSA_KV, HEAD_DIM), new_n[:, 2:4]], axis=0)[None]
        sb_rows = jnp.concatenate([c_sb[pt].reshape(past, 2, SB_HEADS, HEAD_DIM), new_s], axis=0)[None]
        kc = compress(cmp_rows[:, :, 0], p['cmp_pe_k'], p['cmp_w1_k'], p['cmp_w2_k'])
        vc = compress(cmp_rows[:, :, 1], p['cmp_pe_v'], p['cmp_w1_v'], p['cmp_w2_v'])
        o_n = nsa_attend(qn[None], q_pos, gn[None], kc, vc, to_blocks(slc_rows[:, :, 0]),
                         to_blocks(slc_rows[:, :, 1]), wbuf[None, :, 0], wbuf[None, :, 1], kw_pos, rel_bias)
        o_s = sb_attend(qs[None], q_pos, sb_rows[:, :, 0], sb_rows[:, :, 1], k_pos)
        return o_n[0], o_s[0].reshape(S, SB_HEADS * HEAD_DIM)

    o_n, o_s = lax.map(per_seq, (page_table, q_n, g_n, qkv_s[:, :, 0], kv_n, qkv_s[:, :, 1:3], win_buf))
    o_m = mem_attend(q_m, c_mem[:, :, 0], c_mem[:, :, 1]).reshape(B, S, MEM_HEADS * MEM_HD)
    x = x + merge_out(o_n, o_s, o_m, g_m, p)
    x = ffn_sublayer(x, p)
    return x, (kv_n[:, :, 0:2], kv_n[:, :, 2:4], qkv_s[:, :, 1:3], win_buf[:, S:])


def kernel(x_prompt, x_sample, mem_prompt, cache_nsa_cmp, cache_nsa_slc, cache_sb, cache_nsa_win,
           cache_mem, page_table, rel_bias, g_attn, w_in, cmp_pe_k, cmp_w1_k, cmp_w2_k, cmp_pe_v,
           cmp_w1_v, cmp_w2_v, g_mem, w_mem_kv, w_br_nsa, w_br_sb, w_br_mem, w_out, g_ffn, w_router,
           b_router, w_up, b_up, w_down, b_down, g_final):
    depth = g_attn.shape[0]
    xp, xs = x_prompt, x_sample
    st_p, st_s = [], []
    for l in range(depth):
        p = {'g_attn': g_attn[l], 'w_in_segs': split_w_in(w_in[l]),
             'cmp_pe_k': cmp_pe_k[l], 'cmp_w1_k': cmp_w1_k[l], 'cmp_w2_k': cmp_w2_k[l],
             'cmp_pe_v': cmp_pe_v[l], 'cmp_w1_v': cmp_w1_v[l], 'cmp_w2_v': cmp_w2_v[l],
             'g_mem': g_mem[l], 'w_mem_kv': w_mem_kv[l],
             'w_br_nsa': w_br_nsa[l], 'w_br_sb': w_br_sb[l], 'w_br_mem': w_br_mem[l], 'w_out': w_out[l],
             'g_ffn': g_ffn[l], 'w_router': w_router[l], 'b_router': b_router[l],
             'w_up': w_up[l], 'b_up': b_up[l], 'w_down': w_down[l], 'b_down': b_down[l]}
        xp, sp = prompt_layer(xp, mem_prompt, rel_bias, p)
        xs, ss = sample_layer(xs, cache_nsa_cmp[l], cache_nsa_slc[l], cache_sb[l], cache_nsa_win[l],
                              cache_mem[l], page_table, rel_bias, p)
        st_p.append(sp)
        st_s.append(ss)
    y_prompt = rmsnorm(xp, g_final)
    y_sample = rmsnorm(xs, g_final)

    def stacked(states, i):
        return jnp.stack([s[i] for s in states])

    return (y_prompt, y_sample,
            stacked(st_p, 0), stacked(st_p, 1), stacked(st_p, 2), stacked(st_p, 3), stacked(st_p, 4),
            stacked(st_s, 0), stacked(st_s, 1), stacked(st_s, 2), stacked(st_s, 3))
```

```python
import functools
import math

import jax
import jax.numpy as jnp
import numpy as np
from jax import lax
from jax.experimental import pallas as pl
from jax.experimental.pallas import tpu as pltpu

D_MODEL = 1024
PAGE_SIZE = 128
NSA_HEADS = 8
NSA_KV = 2
NSA_HPG = NSA_HEADS // NSA_KV
HEAD_DIM = 64
CMP_LEN = 32
CMP_STRIDE = 16
CMP_HID = 128
SLC_LEN = 64
SLC_TOP = 16
WINDOW = 512
NSA_QBLK = 32
FORCE_SCORE = 1.0e6
SB_HEADS = 8
SB_QBLK = 128
MEM_HEADS = 4
MEM_HD = 128
REL_BUCKETS = 32
REL_MAX_DIST = 2048
N_EXPERTS = 32
TOP_K = 4
D_FF = 1024
SWIGLU_LIMIT = 7.0
SWIGLU_ALPHA = 1.702
MOE_BLOCK = 256
EPS = 1e-6
NEG_INF = -1.0e30

SEG_QN = NSA_HEADS * HEAD_DIM
SEG_KVN = 6 * NSA_KV * HEAD_DIM
SEG_GN = 3 * NSA_HEADS
SEG_QKVS = 3 * SB_HEADS * HEAD_DIM
SEG_QM = MEM_HEADS * MEM_HD
SEG_GM = 3 * D_MODEL
IN_SIZES = (SEG_QN, SEG_KVN, SEG_GN, SEG_QKVS, SEG_QM, SEG_GM)

VMEM_LIMIT = 48 * 1024 * 1024


def _rms_proj_kernel(x_ref, g_ref, *refs, n_seg, sigmoid):
    w_refs = refs[:n_seg]
    o_refs = refs[n_seg:]
    x = x_ref[...]
    h = x * lax.rsqrt(jnp.mean(x * x, axis=-1, keepdims=True) + EPS) * g_ref[...]
    hb = h.astype(jnp.bfloat16)
    for w_ref, o_ref, sig in zip(w_refs, o_refs, sigmoid):
        z = jnp.dot(hb, w_ref[...], preferred_element_type=jnp.float32)
        if sig:
            z = jax.nn.sigmoid(z)
        o_ref[...] = z.astype(o_ref.dtype)


def rms_proj(x2d, g, weights, sigmoid, out_dtypes, tm):
    n, d = x2d.shape
    assert n % tm == 0
    n_seg = len(weights)
    in_specs = [pl.BlockSpec((tm, d), lambda i: (i, 0)),
                pl.BlockSpec((1, d), lambda i: (0, 0))]
    for w in weights:
        in_specs.append(pl.BlockSpec(w.shape, lambda i: (0, 0), pipeline_mode=pl.Buffered(1)))
    out_specs = [pl.BlockSpec((tm, w.shape[1]), lambda i: (i, 0)) for w in weights]
    out_shape = [jax.ShapeDtypeStruct((n, w.shape[1]), dt) for w, dt in zip(weights, out_dtypes)]
    return pl.pallas_call(
        functools.partial(_rms_proj_kernel, n_seg=n_seg, sigmoid=tuple(sigmoid)),
        grid=(n // tm,),
        in_specs=in_specs,
        out_specs=out_specs,
        out_shape=out_shape,
        compiler_params=pltpu.CompilerParams(dimension_semantics=("parallel",),
                                             vmem_limit_bytes=VMEM_LIMIT),
        name="rms_proj",
    )(x2d, g.reshape(1, d), *weights)


def in_proj(x, g_attn, w_segs):
    B, T, D = x.shape
    n = B * T
    tm = 256 if n % 256 == 0 else n
    outs = rms_proj(x.reshape(n, D), g_attn, w_segs,
                    sigmoid=(False, False, True, False, False, True),
                    out_dtypes=(jnp.float32,) * 6, tm=tm)
    q_n, kv_n, g_n, qkv_s, q_m, g_m = outs
    return (q_n.reshape(B, T, NSA_KV, NSA_HPG, HEAD_DIM),
            kv_n.reshape(B, T, 6, NSA_KV, HEAD_DIM),
            g_n.reshape(B, T, NSA_KV, NSA_HPG, 3),
            qkv_s.reshape(B, T, 3, SB_HEADS, HEAD_DIM),
            q_m.reshape(B, T, MEM_HEADS, MEM_HD),
            g_m.reshape(B, T, 3, D_MODEL))


def split_w_in(w_in):
    offs = np.cumsum((0,) + IN_SIZES)
    return [w_in[:, int(a):int(b)].astype(jnp.bfloat16) for a, b in zip(offs[:-1], offs[1:])]


def _nt_dot(a, b):
    return lax.dot_general(a, b, (((1,), (1,)), ((), ())), preferred_element_type=jnp.float32)


def _online_step(s, mask, v_aug, m, acc, cdt):
    h, tq, tk = s.shape
    s = jnp.where(mask[None], s, NEG_INF)
    m_new = jnp.maximum(m, jnp.max(s, axis=-1, keepdims=True))
    p = jnp.where(mask[None], jnp.exp(s - m_new), 0.0)
    alpha = jnp.exp(m - m_new).reshape(h * tq, 1)
    pv = jnp.dot(p.reshape(h * tq, tk).astype(cdt), v_aug, preferred_element_type=jnp.float32)
    return m_new, alpha * acc + pv


def _finish(acc):
    l = acc[:, HEAD_DIM:HEAD_DIM + 1]
    return acc[:, :HEAD_DIM] / jnp.where(l > 0.0, l, 1.0)


def _nsa_prompt_kernel(q_ref, gate_ref, kc_ref, vc_ref, ks_ref, vs_ref, kw_ref, vw_ref, cb_ref, tb_ref,
                       o_ref, *, tq, n_cmp, ns, cdt):
    f32 = jnp.float32
    qi = pl.program_id(2)
    q0 = qi * tq
    tk = tq
    hpg = NSA_HPG
    rows = hpg * tq
    q = q_ref[0].reshape(rows, HEAD_DIM)

    kc = kc_ref[0, 0]
    ncp = kc.shape[0]
    qpos_c = q0 + lax.broadcasted_iota(jnp.int32, (tq, ncp), 0)
    cidx = lax.broadcasted_iota(jnp.int32, (tq, ncp), 1)
    cmask = (cidx * CMP_STRIDE + (CMP_LEN - 1) <= qpos_c) & (cidx < n_cmp)
    s_c = (_nt_dot(q, kc) + cb_ref[0, 0]).reshape(hpg, tq, ncp)
    s_c = jnp.where(cmask[None], s_c, NEG_INF)
    m_c = jnp.max(s_c, axis=-1, keepdims=True)
    p_c = jnp.where(cmask[None], jnp.exp(s_c - m_c), 0.0)
    l_c = jnp.sum(p_c, axis=-1, keepdims=True)
    p_c = p_c / jnp.where(l_c > 0.0, l_c, 1.0)
    o_c = jnp.dot(p_c.reshape(rows, ncp).astype(cdt), vc_ref[0, 0], preferred_element_type=f32)

    p_sum = p_c[0]
    for h in range(1, hpg):
        p_sum = p_sum + p_c[h]
    jb = lax.broadcasted_iota(jnp.int32, (ns, ncp), 0)
    cc = lax.broadcasted_iota(jnp.int32, (ns, ncp), 1)
    ov_t = ((cc * CMP_STRIDE < jb * SLC_LEN + SLC_LEN)
            & (cc * CMP_STRIDE + (CMP_LEN - 1) >= jb * SLC_LEN)).astype(cdt)
    p_hi = p_sum.astype(cdt)
    p_lo = (p_sum - p_hi.astype(f32)).astype(cdt)
    imp = _nt_dot(ov_t, p_hi) + _nt_dot(ov_t, p_lo)
    blk = lax.broadcasted_iota(jnp.int32, (ns, tq), 0)
    qp_t = q0 + lax.broadcasted_iota(jnp.int32, (ns, tq), 1)
    cur = qp_t // SLC_LEN
    forced = (blk == 0) | (blk == cur) | (blk == cur - 1)
    imp = jnp.where(forced, FORCE_SCORE, imp)
    imp = jnp.where(blk * SLC_LEN <= qp_t, imp, -1.0)
    cnt = jnp.zeros((ns, tq), f32)
    for i in range(ns):
        row = imp[i:i + 1, :]
        ahead = (row > imp) | ((row == imp) & (blk > i))
        cnt = cnt + jnp.where(ahead, 1.0, 0.0)
    sel_t = jnp.where((cnt < float(min(SLC_TOP, ns))) & (imp >= 0.0), 1.0, 0.0)
    sel = sel_t.T.astype(cdt)

    qpos = q0 + lax.broadcasted_iota(jnp.int32, (tq, tk), 0)
    kcol = lax.broadcasted_iota(jnp.int32, (tq, tk), 1)
    eb = lax.broadcasted_iota(jnp.int32, (ns, tk), 0)
    ek = lax.broadcasted_iota(jnp.int32, (ns, tk), 1)
    m0 = jnp.full((hpg, tq, 1), NEG_INF, f32)
    a0 = jnp.zeros((rows, 2 * HEAD_DIM), f32)

    def sel_body(kb, carry):
        m, acc = carry
        k0 = pl.multiple_of(kb * tk, tk)
        s = (_nt_dot(q, ks_ref[0, 0, pl.ds(k0, tk), :]) + tb_ref[0, qi - kb]).reshape(hpg, tq, tk)
        expand = jnp.where(eb == (k0 + ek) // SLC_LEN, 1.0, 0.0).astype(cdt)
        picked = jnp.dot(sel, expand, preferred_element_type=f32)
        mask = (picked > 0.5) & (k0 + kcol <= qpos)
        return _online_step(s, mask, vs_ref[0, 0, pl.ds(k0, tk), :], m, acc, cdt)

    _, acc_s = lax.fori_loop(0, qi + 1, sel_body, (m0, a0))
    o_s = _finish(acc_s)

    def win_body(kb, carry):
        m, acc = carry
        k0 = pl.multiple_of(kb * tk, tk)
        s = (_nt_dot(q, kw_ref[0, 0, pl.ds(k0, tk), :]) + tb_ref[0, qi - kb]).reshape(hpg, tq, tk)
        dist = qpos - (k0 + kcol)
        mask = (dist >= 0) & (dist < WINDOW)
        return _online_step(s, mask, vw_ref[0, 0, pl.ds(k0, tk), :], m, acc, cdt)

    _, acc_w = lax.fori_loop(jnp.maximum(qi - WINDOW // tk, 0), qi + 1, win_body, (m0, a0))
    o_w = _finish(acc_w)

    gates = gate_ref[0, 0]
    outs = []
    for h in range(hpg):
        r = slice(h * tq, (h + 1) * tq)
        outs.append(gates[:, 3 * h:3 * h + 1] * o_c[r]
                    + gates[:, 3 * h + 1:3 * h + 2] * o_s[r]
                    + gates[:, 3 * h + 2:3 * h + 3] * o_w[r])
    o_ref[0] = jnp.concatenate(outs, axis=-1).astype(o_ref.dtype)


def _aug_ones(v):
    return jnp.concatenate([v, jnp.ones_like(v)], axis=-1)


def nsa_bias_tables(rel_bias, t_len, tq):
    nq = t_len // tq
    ncp = t_len // CMP_STRIDE
    bias_d = rel_bias[rel_bucket(jnp.arange(t_len, dtype=jnp.int32))]
    bias_d = bias_d.reshape(t_len, NSA_KV, NSA_HPG).transpose(1, 2, 0)
    i = jnp.arange(tq)[:, None]
    d_t = jnp.arange(nq)[:, None, None] * tq + i[None] - jnp.arange(tq)[None, None, :]
    tb = bias_d[:, :, jnp.clip(d_t, 0, t_len - 1)]
    tb = tb.transpose(0, 2, 1, 3, 4).reshape(NSA_KV, nq, NSA_HPG * tq, tq)
    c_end = jnp.arange(ncp) * CMP_STRIDE + (CMP_LEN - 1)
    d_c = jnp.arange(nq)[:, None, None] * tq + i[None] - c_end[None, None, :]
    cb = bias_d[:, :, jnp.clip(d_c, 0, t_len - 1)]
    cb = cb.transpose(0, 2, 1, 3, 4).reshape(NSA_KV, nq, NSA_HPG * tq, ncp)
    return tb, cb


def nsa_prompt_pallas(q, gates, kc, vc, k_slc, v_slc, k_win, v_win, rel_bias, *, tq=128, cdt=jnp.bfloat16):
    B, T = q.shape[:2]
    n_cmp = kc.shape[1]
    ncp = T // CMP_STRIDE
    ns = T // SLC_LEN
    nq = T // tq
    rows = NSA_HPG * tq
    tb, cb = nsa_bias_tables(rel_bias, T, tq)
    to_g = lambda a: a.astype(cdt).transpose(0, 2, 1, 3)
    qh = q.astype(cdt).reshape(B, T, NSA_HEADS, HEAD_DIM).transpose(0, 2, 1, 3)
    gt = gates.astype(jnp.float32).reshape(B, T, NSA_KV, NSA_HPG * 3).transpose(0, 2, 1, 3)
    pad_c = ((0, 0), (0, ncp - n_cmp), (0, 0), (0, 0))
    kcg = to_g(jnp.pad(kc, pad_c))
    vcg = to_g(jnp.pad(vc, pad_c))
    ksg, kwg = to_g(k_slc), to_g(k_win)
    vsg, vwg = _aug_ones(to_g(v_slc)), _aug_ones(to_g(v_win))
    per_bg = lambda w: pl.BlockSpec((1, 1, T, w), lambda g, b, i: (b, g, 0, 0))
    per_bg_c = lambda w: pl.BlockSpec((1, 1, ncp, w), lambda g, b, i: (b, g, 0, 0))
    return pl.pallas_call(
        functools.partial(_nsa_prompt_kernel, tq=tq, n_cmp=n_cmp, ns=ns, cdt=cdt),
        grid=(NSA_KV, B, nq),
        in_specs=[
            pl.BlockSpec((1, NSA_HPG, tq, HEAD_DIM), lambda g, b, i: (b, g, i, 0)),
            pl.BlockSpec((1, 1, tq, NSA_HPG * 3), lambda g, b, i: (b, g, i, 0)),
            per_bg_c(HEAD_DIM), per_bg_c(HEAD_DIM),
            per_bg(HEAD_DIM), per_bg(2 * HEAD_DIM), per_bg(HEAD_DIM), per_bg(2 * HEAD_DIM),
            pl.BlockSpec((1, 1, rows, ncp), lambda g, b, i: (g, i, 0, 0)),
            pl.BlockSpec((1, nq, rows, tq), lambda g, b, i: (g, 0, 0, 0), pipeline_mode=pl.Buffered(1)),
        ],
        out_specs=pl.BlockSpec((1, tq, NSA_HPG * HEAD_DIM), lambda g, b, i: (b, i, g)),
        out_shape=jax.ShapeDtypeStruct((B, T, NSA_HEADS * HEAD_DIM), cdt),
        compiler_params=pltpu.CompilerParams(dimension_semantics=("parallel", "parallel", "parallel"),
                                             vmem_limit_bytes=VMEM_LIMIT),
        name="nsa_prompt",
    )(qh, gt, kcg, vcg, ksg, vsg, kwg, vwg, cb, tb)


SB_DEAD_LOG = -104.0


def _split3(x, cdt):
    hi = x.astype(cdt)
    r = x - hi.astype(jnp.float32)
    mid = r.astype(cdt)
    lo = (r - mid.astype(jnp.float32)).astype(cdt)
    return hi, mid, lo


def _sb_tile(z, causal, carry, v, tri, cdt):
    h, tq, tk = z.shape
    sp = jnp.maximum(z, 0.0) + jnp.log1p(jnp.exp(-jnp.abs(z)))
    l1m = jnp.where(causal[None], -sp, 0.0).reshape(h * tq, tk)
    suffix = None
    for part in _split3(l1m, cdt):
        d = jnp.dot(part, tri, preferred_element_type=jnp.float32)
        suffix = d if suffix is None else suffix + d
    surv = (suffix + carry).reshape(h, tq, tk)
    a = jnp.where(causal[None], jnp.exp(z - sp + surv), 0.0)
    pv = jnp.einsum('hqk,hkd->hqd', a.astype(cdt), v, preferred_element_type=jnp.float32)
    return pv, carry + suffix[:, 0:1] + l1m[:, 0:1]


def _sb_prompt_kernel(q_ref, k_ref, v_ref, o_ref, *, tq, hs, cdt):
    f32 = jnp.float32
    qi = pl.program_id(2)
    q0 = qi * tq
    tk = tq
    q = q_ref[0, 0]
    qpos = q0 + lax.broadcasted_iota(jnp.int32, (tq, tk), 0)
    kcol = lax.broadcasted_iota(jnp.int32, (tq, tk), 1)
    tri = jnp.where(lax.broadcasted_iota(jnp.int32, (tk, tk), 0) > lax.broadcasted_iota(jnp.int32, (tk, tk), 1),
                    1.0, 0.0).astype(cdt)

    def cond(state):
        kb, live, _, _ = state
        return (kb >= 0) & (live > SB_DEAD_LOG)

    def body(state):
        kb, _, carry, acc = state
        k0 = pl.multiple_of(kb * tk, tk)
        k = k_ref[0, 0, :, pl.ds(k0, tk), :]
        v = v_ref[0, 0, :, pl.ds(k0, tk), :]
        z = jnp.einsum('hqd,hkd->hqk', q, k, preferred_element_type=f32)
        pv, carry = _sb_tile(z, k0 + kcol < qpos, carry, v, tri, cdt)
        return kb - 1, jnp.max(carry), carry, acc + pv

    init = (qi, jnp.float32(0.0), jnp.zeros((hs * tq, 1), f32), jnp.zeros((hs, tq, HEAD_DIM), f32))
    _, _, _, acc = lax.while_loop(cond, body, init)
    o_ref[0] = jnp.concatenate([acc[h] for h in range(hs)], axis=-1).astype(o_ref.dtype)


def sb_prompt_pallas(qkv, *, tq=128, hs=4, cdt=jnp.bfloat16):
    B, T = qkv.shape[:2]
    x = qkv.astype(cdt).transpose(0, 2, 3, 1, 4)
    full = lambda j: pl.BlockSpec((1, 1, hs, T, HEAD_DIM), lambda b, g, i: (b, j, g, 0, 0))
    return pl.pallas_call(
        functools.partial(_sb_prompt_kernel, tq=tq, hs=hs, cdt=cdt),
        grid=(B, SB_HEADS // hs, T // tq),
        in_specs=[pl.BlockSpec((1, 1, hs, tq, HEAD_DIM), lambda b, g, i: (b, 0, g, i, 0)), full(1), full(2)],
        out_specs=pl.BlockSpec((1, tq, hs * HEAD_DIM), lambda b, g, i: (b, i, g)),
        out_shape=jax.ShapeDtypeStruct((B, T, SB_HEADS * HEAD_DIM), cdt),
        compiler_params=pltpu.CompilerParams(dimension_semantics=("parallel", "parallel", "parallel"),
                                             vmem_limit_bytes=VMEM_LIMIT),
        name="sb_prompt",
    )(x, x, x)


def _moe_kernel(blk_e_ref, n_used_ref, tok_ref, dst_ref, rw_ref, h_hbm, wup_ref, bup_ref, wdn_ref, bdn_ref,
                parts_hbm, xbuf, obuf, sem_in, sem_out, *, blk, n_assign, cdt):
    j = pl.program_id(0)

    def gather_copy(r):
        return pltpu.make_async_copy(h_hbm.at[pl.ds(tok_ref[0, 0, r], 1)], xbuf.at[pl.ds(r, 1)], sem_in)

    def scatter_copy(r):
        return pltpu.make_async_copy(obuf.at[pl.ds(r, 1)], parts_hbm.at[pl.ds(dst_ref[0, 0, r], 1)], sem_out)

    def each_row(fn, real_only):
        def body(r, c):
            if real_only:
                @pl.when(dst_ref[0, 0, r] < n_assign)
                def _():
                    fn(r)
            else:
                fn(r)
            return c
        lax.fori_loop(0, blk, body, 0, unroll=8)

    @pl.when(j < n_used_ref[0])
    def _():
        each_row(lambda r: gather_copy(r).start(), False)
        each_row(lambda r: gather_copy(r).wait(), False)
        x = xbuf[...].astype(cdt)
        gu = jnp.dot(x, wup_ref[0], preferred_element_type=jnp.float32) + bup_ref[0]
        gate = jnp.minimum(gu[:, :D_FF], SWIGLU_LIMIT)
        up = jnp.clip(gu[:, D_FF:], -SWIGLU_LIMIT, SWIGLU_LIMIT)
        act = (up + 1.0) * gate * jax.nn.sigmoid(SWIGLU_ALPHA * gate)
        out = jnp.dot(act.astype(cdt), wdn_ref[0], preferred_element_type=jnp.float32) + bdn_ref[0]
        obuf[...] = out * rw_ref[...]
        each_row(lambda r: scatter_copy(r).start(), True)
        each_row(lambda r: scatter_copy(r).wait(), True)


def moe_ffn_pallas(h_pad, blk_e, n_used, row_tok, row_dst, row_w, w_up, b_up, w_down, b_down, *, blk, n_assign,
                   cdt=jnp.bfloat16):
    n_rows = row_tok.shape[0]
    n_blocks = n_rows // blk
    d = h_pad.shape[1]
    smem_rows = lambda: pl.BlockSpec((1, 1, blk), lambda j, be, nu: (j, 0, 0), memory_space=pltpu.SMEM)
    grid_spec = pltpu.PrefetchScalarGridSpec(
        num_scalar_prefetch=2,
        grid=(n_blocks,),
        in_specs=[
            smem_rows(), smem_rows(),
            pl.BlockSpec((blk, 1), lambda j, be, nu: (j, 0)),
            pl.BlockSpec(memory_space=pl.ANY),
            pl.BlockSpec((1, d, 2 * D_FF), lambda j, be, nu: (be[j], 0, 0)),
            pl.BlockSpec((1, 1, 2 * D_FF), lambda j, be, nu: (be[j], 0, 0)),
            pl.BlockSpec((1, D_FF, d), lambda j, be, nu: (be[j], 0, 0)),
            pl.BlockSpec((1, 1, d), lambda j, be, nu: (be[j], 0, 0)),
        ],
        out_specs=pl.BlockSpec(memory_space=pl.ANY),
        scratch_shapes=[pltpu.VMEM((blk, d), jnp.float32), pltpu.VMEM((blk, d), jnp.float32),
                        pltpu.SemaphoreType.DMA(()), pltpu.SemaphoreType.DMA(())],
    )
    return pl.pallas_call(
        functools.partial(_moe_kernel, blk=blk, n_assign=n_assign, cdt=cdt),
        grid_spec=grid_spec,
        out_shape=jax.ShapeDtypeStruct((n_assign, d), jnp.float32),
        compiler_params=pltpu.CompilerParams(dimension_semantics=("arbitrary",), vmem_limit_bytes=VMEM_LIMIT),
        name="moe_ffn",
    )(blk_e, n_used, row_tok.reshape(n_blocks, 1, blk), row_dst.reshape(n_blocks, 1, blk),
      row_w.reshape(n_rows, 1), h_pad, w_up.astype(cdt), b_up.reshape(N_EXPERTS, 1, 2 * D_FF),
      w_down.astype(cdt), b_down.reshape(N_EXPERTS, 1, d))


def _combine_norm_kernel(x_ref, parts_ref, g_ref, o_ref, *, normalize):
    x = x_ref[...]
    d = x.shape[1]
    for k in range(TOP_K):
        x = x + parts_ref[:, k * d:(k + 1) * d]
    if normalize:
        x = x * lax.rsqrt(jnp.mean(x * x, axis=-1, keepdims=True) + EPS) * g_ref[...]
    o_ref[...] = x


def combine_norm_pallas(x, parts, g_final, tm, normalize):
    n, d = x.shape
    return pl.pallas_call(
        functools.partial(_combine_norm_kernel, normalize=normalize),
        grid=(n // tm,),
        in_specs=[pl.BlockSpec((tm, d), lambda i: (i, 0)), pl.BlockSpec((tm, TOP_K * d), lambda i: (i, 0)),
                  pl.BlockSpec((1, d), lambda i: (0, 0))],
        out_specs=pl.BlockSpec((tm, d), lambda i: (i, 0)),
        out_shape=jax.ShapeDtypeStruct((n, d), jnp.float32),
        compiler_params=pltpu.CompilerParams(dimension_semantics=("parallel",), vmem_limit_bytes=VMEM_LIMIT),
        name="combine_norm",
    )(x, parts, g_final.reshape(1, d))


def ffn_block(x, p, g_final, last):
    lead = x.shape[:-1]
    x2 = x.reshape(-1, D_MODEL)
    N = x2.shape[0]
    hf = rmsnorm(x2, p['g_ffn'])
    logits = jnp.einsum('nd,de->ne', hf, p['w_router'], preferred_element_type=jnp.float32) + p['b_router']
    top_l, top_e = lax.top_k(logits, TOP_K)
    top_w = jax.nn.softmax(top_l, axis=-1)
    n_assign = N * TOP_K
    blk = int(min(MOE_BLOCK, max(8, n_assign // N_EXPERTS)))
    n_rows = (n_assign // blk + N_EXPERTS) * blk
    n_blocks = n_rows // blk
    flat_e = top_e.reshape(-1)
    order = jnp.argsort(flat_e).astype(jnp.int32)
    se = flat_e[order]
    counts = jnp.bincount(flat_e, length=N_EXPERTS)
    start = jnp.cumsum(counts) - counts
    padded = (counts + blk - 1) // blk * blk
    p_end = jnp.cumsum(padded)
    p_start = p_end - padded
    dest = p_start[se] + jnp.arange(n_assign) - start[se]
    row_tok = jnp.full((n_rows,), N, jnp.int32).at[dest].set(order // TOP_K)
    row_dst = jnp.full((n_rows,), n_assign, jnp.int32).at[dest].set(order)
    row_w = jnp.zeros((n_rows,), jnp.float32).at[dest].set(top_w.reshape(-1)[order])
    blk_e = jnp.minimum(jnp.searchsorted(p_end, jnp.arange(n_blocks) * blk, side='right'),
                        N_EXPERTS - 1).astype(jnp.int32)
    n_used = (p_end[-1] // blk).astype(jnp.int32).reshape(1)
    h_pad = jnp.concatenate([hf, jnp.zeros((1, D_MODEL), hf.dtype)], axis=0)
    parts = moe_ffn_pallas(h_pad, blk_e, n_used, row_tok, row_dst, row_w, p['w_up'], p['b_up'],
                           p['w_down'], p['b_down'], blk=blk, n_assign=n_assign)
    tm = 256 if N % 256 == 0 else N
    y = combine_norm_pallas(x2, parts.reshape(N, TOP_K * D_MODEL), g_final, tm, normalize=last)
    return y.reshape(*lead, D_MODEL)


def rmsnorm(x, g):
    xf = x.astype(jnp.float32)
    xf = xf * lax.rsqrt(jnp.mean(xf * xf, axis=-1, keepdims=True) + EPS)
    return (xf * g.astype(jnp.float32)).astype(x.dtype)


def masked_softmax(s, mask):
    p = jax.nn.softmax(jnp.where(mask, s, NEG_INF), axis=-1)
    return jnp.where(mask, p, 0.0)


def rel_bucket(dist):
    n = jnp.maximum(dist, 0)
    max_exact = REL_BUCKETS // 2
    nf = jnp.maximum(n, 1).astype(jnp.float32)
    large = max_exact + (jnp.log(nf / max_exact) / math.log(REL_MAX_DIST / max_exact)
                         * (REL_BUCKETS - max_exact)).astype(jnp.int32)
    return jnp.where(n < max_exact, n, jnp.minimum(large, REL_BUCKETS - 1))


def compress(x, pe, w1, w2):
    B, T = x.shape[:2]
    R = CMP_LEN // CMP_STRIDE
    nh = T // CMP_STRIDE
    nc = nh - R + 1
    halves = x[:, :nh * CMP_STRIDE].reshape(B, nh, CMP_STRIDE, NSA_KV, HEAD_DIM)
    w1h = w1.reshape(R, CMP_STRIDE, HEAD_DIM, CMP_HID)
    pre = pe.reshape(-1) @ w1
    for r in range(R):
        pre = pre + jnp.einsum('bnsgd,sdh->bngh', halves[:, r:r + nc], w1h[r])
    return jax.nn.gelu(pre) @ w2


def to_blocks(x):
    B, T = x.shape[:2]
    ns = -(-T // SLC_LEN)
    x = jnp.pad(x, ((0, 0), (0, ns * SLC_LEN - T), (0, 0), (0, 0)))
    return x.reshape(B, ns, SLC_LEN, NSA_KV, HEAD_DIM).transpose(0, 3, 1, 2, 4)


def nsa_attend(q, q_pos, gates, kc, vc, ks_blk, vs_blk, kw, vw, kw_pos, rel_bias):
    B, Q = q.shape[:2]
    NC = kc.shape[1]
    NS = ks_blk.shape[2]
    KW = kw.shape[1]
    scale = HEAD_DIM ** -0.5
    f32 = jnp.float32
    c_start = jnp.arange(NC, dtype=jnp.int32) * CMP_STRIDE
    c_end = c_start + (CMP_LEN - 1)
    c_mask = (c_end[None, :] <= q_pos[:, None])[:, None, None, :]
    c_bias = rel_bias[rel_bucket(q_pos[:, None] - c_end[None, :])].reshape(
        Q, NC, NSA_KV, NSA_HPG).transpose(0, 2, 3, 1)
    s_c = jnp.einsum('bqghd,bcgd->bqghc', q, kc, preferred_element_type=f32) * scale + c_bias
    p_c = masked_softmax(s_c, c_mask)
    o_c = jnp.einsum('bqghc,bcgd->bqghd', p_c.astype(vc.dtype), vc)
    s_start = jnp.arange(NS, dtype=jnp.int32) * SLC_LEN
    overlap = ((c_start[:, None] < s_start[None, :] + SLC_LEN)
               & (c_end[:, None] >= s_start[None, :])).astype(f32)
    imp = jnp.einsum('bqghc,cj->bqgj', p_c, overlap)
    blk = jnp.arange(NS, dtype=jnp.int32)[None, :]
    cur = (q_pos // SLC_LEN)[:, None]
    forced = (blk == 0) | (blk == cur) | (blk == cur - 1)
    valid = s_start[None, :] <= q_pos[:, None]
    imp = jnp.where(forced[None, :, None, :], FORCE_SCORE, imp)
    imp = jnp.where(valid[None, :, None, :], imp, -1.0)
    n_top = min(SLC_TOP, NS)
    top_val, idx = lax.top_k(imp, n_top)
    bi = jnp.arange(B)[:, None, None, None]
    gi = jnp.arange(NSA_KV)[None, None, :, None]
    k_sel = ks_blk[bi, gi, idx]
    v_sel = vs_blk[bi, gi, idx]
    k_pos = idx[..., None] * SLC_LEN + jnp.arange(SLC_LEN, dtype=jnp.int32)
    qp = q_pos[None, :, None, None, None]
    s_mask = (top_val >= 0)[..., None] & (k_pos <= qp)
    bias_g = rel_bias.reshape(REL_BUCKETS, NSA_KV, NSA_HPG).transpose(1, 0, 2)
    s_bias = bias_g[gi[..., None], rel_bucket(qp - k_pos)]
    n_keys = n_top * SLC_LEN
    s_bias = s_bias.transpose(0, 1, 2, 5, 3, 4).reshape(B, Q, NSA_KV, NSA_HPG, n_keys)
    s_s = jnp.einsum('bqghd,bqgnkd->bqghnk', q, k_sel, preferred_element_type=f32).reshape(
        B, Q, NSA_KV, NSA_HPG, n_keys) * scale + s_bias
    p_s = masked_softmax(s_s, s_mask.reshape(B, Q, NSA_KV, 1, n_keys))
    o_s = jnp.einsum('bqghk,bqgkd->bqghd', p_s.astype(v_sel.dtype),
                     v_sel.reshape(B, Q, NSA_KV, n_keys, HEAD_DIM))
    w_dist = q_pos[:, None] - kw_pos[None, :]
    w_mask = ((w_dist >= 0) & (w_dist < WINDOW) & (kw_pos[None, :] >= 0))[:, None, None, :]
    w_bias = rel_bias[rel_bucket(w_dist)].reshape(Q, KW, NSA_KV, NSA_HPG).transpose(0, 2, 3, 1)
    s_w = jnp.einsum('bqghd,bkgd->bqghk', q, kw, preferred_element_type=f32) * scale + w_bias
    p_w = masked_softmax(s_w, w_mask)
    o_w = jnp.einsum('bqghk,bkgd->bqghd', p_w.astype(vw.dtype), vw)
    g = gates.astype(o_c.dtype)
    o = g[..., 0:1] * o_c + g[..., 1:2] * o_s + g[..., 2:3] * o_w
    return o.reshape(B, Q, NSA_HEADS * HEAD_DIM)


def nsa_prompt(q, gates, kc, vc, ks_blk, vs_blk, k_win, v_win, rel_bias):
    B, T = q.shape[:2]
    pad = ((0, 0), (WINDOW, 0), (0, 0), (0, 0))
    kw_pad = jnp.pad(k_win, pad)
    vw_pad = jnp.pad(v_win, pad)

    def block(i):
        t0 = i * NSA_QBLK
        qb = lax.dynamic_slice_in_dim(q, t0, NSA_QBLK, axis=1)
        gb = lax.dynamic_slice_in_dim(gates, t0, NSA_QBLK, axis=1)
        kw = lax.dynamic_slice_in_dim(kw_pad, t0, WINDOW + NSA_QBLK, axis=1)
        vw = lax.dynamic_slice_in_dim(vw_pad, t0, WINDOW + NSA_QBLK, axis=1)
        q_pos = t0 + jnp.arange(NSA_QBLK, dtype=jnp.int32)
        kw_pos = t0 - WINDOW + jnp.arange(WINDOW + NSA_QBLK, dtype=jnp.int32)
        return nsa_attend(qb, q_pos, gb, kc, vc, ks_blk, vs_blk, kw, vw, kw_pos, rel_bias)

    out = lax.map(block, jnp.arange(T // NSA_QBLK, dtype=jnp.int32))
    return out.transpose(1, 0, 2, 3).reshape(B, T, NSA_HEADS * HEAD_DIM)


def sb_attend(q, q_pos, k, v, k_pos):
    z = jnp.einsum('bqhd,bkhd->bhqk', q, k, preferred_element_type=jnp.float32) * (HEAD_DIM ** -0.5)
    causal = k_pos[None, :] < q_pos[:, None]
    log_1m = jnp.where(causal, -jax.nn.softplus(z), 0.0)
    surv = lax.cumsum(log_1m, axis=3, reverse=True) - log_1m
    a = jnp.where(causal, jnp.exp(jax.nn.log_sigmoid(z) + surv), 0.0)
    return jnp.einsum('bhqk,bkhd->bqhd', a.astype(v.dtype), v)


def sb_prompt(q, k, v):
    B, T = q.shape[:2]
    k_pos = jnp.arange(T, dtype=jnp.int32)

    def block(i):
        t0 = i * SB_QBLK
        qb = lax.dynamic_slice_in_dim(q, t0, SB_QBLK, axis=1)
        return sb_attend(qb, t0 + jnp.arange(SB_QBLK, dtype=jnp.int32), k, v, k_pos)

    out = lax.map(block, jnp.arange(T // SB_QBLK, dtype=jnp.int32))
    return out.transpose(1, 0, 2, 3, 4).reshape(B, T, SB_HEADS * HEAD_DIM)


def mem_attend(q, mk, mv):
    s = jnp.einsum('bqhd,bmhd->bhqm', q, mk, preferred_element_type=jnp.float32) * (MEM_HD ** -0.5)
    p = jax.nn.softmax(s, axis=-1)
    return jnp.einsum('bhqm,bmhd->bqhd', p.astype(mv.dtype), mv)


def merge_out(o_n, o_s, o_m, g_m, p):
    y = (g_m[:, :, 0] * (o_n @ p['w_br_nsa'])
         + g_m[:, :, 1] * (o_s @ p['w_br_sb'])
         + g_m[:, :, 2] * (o_m @ p['w_br_mem']))
    return y @ p['w_out']


def moe(h, w_router, b_router, w_up, b_up, w_down, b_down):
    lead = h.shape[:-1]
    xf = h.reshape(-1, D_MODEL)
    N = xf.shape[0]
    logits = jnp.einsum('nd,de->ne', xf, w_router, preferred_element_type=jnp.float32) + b_router.astype(jnp.float32)
    top_l, top_e = lax.top_k(logits, TOP_K)
    top_w = jax.nn.softmax(top_l, axis=-1)
    n_assign = N * TOP_K
    blk = int(min(MOE_BLOCK, max(8, n_assign // N_EXPERTS)))
    n_rows = (n_assign // blk + N_EXPERTS) * blk
    n_blocks = n_rows // blk
    flat_e = top_e.reshape(-1)
    flat_t = jnp.arange(n_assign, dtype=jnp.int32) // TOP_K
    flat_w = top_w.reshape(-1)
    order = jnp.argsort(flat_e)
    se = flat_e[order]
    counts = jnp.bincount(flat_e, length=N_EXPERTS)
    start = jnp.cumsum(counts) - counts
    padded = (counts + blk - 1) // blk * blk
    p_end = jnp.cumsum(padded)
    p_start = p_end - padded
    dest = p_start[se] + jnp.arange(n_assign) - start[se]
    row_tok = jnp.full((n_rows,), N, jnp.int32).at[dest].set(flat_t[order])
    row_w = jnp.zeros((n_rows,), jnp.float32).at[dest].set(flat_w[order])
    blk_e = jnp.minimum(jnp.searchsorted(p_end, jnp.arange(n_blocks) * blk, side='right'), N_EXPERTS - 1)
    x_pad = jnp.concatenate([xf, jnp.zeros((1, D_MODEL), xf.dtype)], axis=0)

    def run(args):
        e, toks = args
        gu = x_pad[toks] @ w_up[e] + b_up[e]
        gate = jnp.minimum(gu[:, :D_FF], SWIGLU_LIMIT)
        up = jnp.clip(gu[:, D_FF:], -SWIGLU_LIMIT, SWIGLU_LIMIT)
        act = (up + 1.0) * gate * jax.nn.sigmoid(SWIGLU_ALPHA * gate)
        return act @ w_down[e] + b_down[e]

    out = lax.map(run, (blk_e, row_tok.reshape(n_blocks, blk)))
    out = out.reshape(n_rows, D_MODEL) * row_w[:, None].astype(out.dtype)
    y = jnp.zeros((N + 1, D_MODEL), out.dtype).at[row_tok].add(out)[:N]
    return y.reshape(*lead, D_MODEL)


def ffn_sublayer(x, p):
    return x + moe(rmsnorm(x, p['g_ffn']), p['w_router'], p['b_router'],
                   p['w_up'], p['b_up'], p['w_down'], p['b_down'])


def prompt_layer(x, mem, rel_bias, p):
    B, T = x.shape[:2]
    q_n, kv_n, g_n, qkv_s, q_m, g_m = in_proj(x, p['g_attn'], p['w_in_segs'])
    kc = compress(kv_n[:, :, 0], p['cmp_pe_k'], p['cmp_w1_k'], p['cmp_w2_k'])
    vc = compress(kv_n[:, :, 1], p['cmp_pe_v'], p['cmp_w1_v'], p['cmp_w2_v'])
    att_scale = HEAD_DIM ** -0.5
    o_n = nsa_prompt_pallas(q_n * att_scale, g_n, kc, vc, kv_n[:, :, 2], kv_n[:, :, 3],
                            kv_n[:, :, 4], kv_n[:, :, 5], rel_bias)
    o_s = sb_prompt_pallas(qkv_s * jnp.array([att_scale, 1.0, 1.0], jnp.float32).reshape(1, 1, 3, 1, 1))
    mem_kv = (rmsnorm(mem, p['g_mem']) @ p['w_mem_kv']).reshape(B, mem.shape[1], 2, MEM_HEADS, MEM_HD)
    o_m = mem_attend(q_m, mem_kv[:, :, 0], mem_kv[:, :, 1]).reshape(B, T, MEM_HEADS * MEM_HD)
    x = x + merge_out(o_n, o_s, o_m, g_m, p)
    wb = min(WINDOW, T)
    return x, (kv_n[:, :, 0:2], kv_n[:, :, 2:4], qkv_s[:, :, 1:3], kv_n[:, T - wb:, 4:6], mem_kv)


def sample_layer(x, c_cmp, c_slc, c_sb, c_win, c_mem, page_table, rel_bias, p):
    B, S = x.shape[:2]
    past = page_table.shape[1] * PAGE_SIZE
    q_n, kv_n, g_n, qkv_s, q_m, g_m = in_proj(x, p['g_attn'], p['w_in_segs'])
    wb = c_win.shape[1]
    win_buf = jnp.concatenate([c_win, kv_n[:, :, 4:6]], axis=1)
    q_pos = past + jnp.arange(S, dtype=jnp.int32)
    kw_pos = past - wb + jnp.arange(wb + S, dtype=jnp.int32)
    k_pos = jnp.arange(past + S, dtype=jnp.int32)

    def per_seq(args):
        pt, qn, gn, qs, new_n, new_s, wbuf = args
        cmp_rows = jnp.concatenate([c_cmp[pt].reshape(past, 2, NSA_KV, HEAD_DIM), new_n[:, 0:2]], axis=0)[None]
        slc_rows = jnp.concatenate([c_slc[pt].reshape(past, 2, NSA_KV, HEAD_DIM), new_n[:, 2:4]], axis=0)[None]
        sb_rows = jnp.concatenate([c_sb[pt].reshape(past, 2, SB_HEADS, HEAD_DIM), new_s], axis=0)[None]
        kc = compress(cmp_rows[:, :, 0], p['cmp_pe_k'], p['cmp_w1_k'], p['cmp_w2_k'])
        vc = compress(cmp_rows[:, :, 1], p['cmp_pe_v'], p['cmp_w1_v'], p['cmp_w2_v'])
        o_n = nsa_attend(qn[None], q_pos, gn[None], kc, vc, to_blocks(slc_rows[:, :, 0]),
                         to_blocks(slc_rows[:, :, 1]), wbuf[None, :, 0], wbuf[None, :, 1], kw_pos, rel_bias)
        o_s = sb_attend(qs[None], q_pos, sb_rows[:, :, 0], sb_rows[:, :, 1], k_pos)
        return o_n[0], o_s[0].reshape(S, SB_HEADS * HEAD_DIM)

    o_n, o_s = lax.map(per_seq, (page_table, q_n, g_n, qkv_s[:, :, 0], kv_n, qkv_s[:, :, 1:3], win_buf))
    o_m = mem_attend(q_m, c_mem[:, :, 0], c_mem[:, :, 1]).reshape(B, S, MEM_HEADS * MEM_HD)
    x = x + merge_out(o_n, o_s, o_m, g_m, p)
    return x, (kv_n[:, :, 0:2], kv_n[:, :, 2:4], qkv_s[:, :, 1:3], win_buf[:, S:])


def kernel(x_prompt, x_sample, mem_prompt, cache_nsa_cmp, cache_nsa_slc, cache_sb, cache_nsa_win,
           cache_mem, page_table, rel_bias, g_attn, w_in, cmp_pe_k, cmp_w1_k, cmp_w2_k, cmp_pe_v,
           cmp_w1_v, cmp_w2_v, g_mem, w_mem_kv, w_br_nsa, w_br_sb, w_br_mem, w_out, g_ffn, w_router,
           b_router, w_up, b_up, w_down, b_down, g_final):
    depth = g_attn.shape[0]
    xp, xs = x_prompt, x_sample
    st_p, st_s = [], []
    for l in range(depth):
        p = {'g_attn': g_attn[l], 'w_in_segs': split_w_in(w_in[l]),
             'cmp_pe_k': cmp_pe_k[l], 'cmp_w1_k': cmp_w1_k[l], 'cmp_w2_k': cmp_w2_k[l],
             'cmp_pe_v': cmp_pe_v[l], 'cmp_w1_v': cmp_w1_v[l], 'cmp_w2_v': cmp_w2_v[l],
             'g_mem': g_mem[l], 'w_mem_kv': w_mem_kv[l],
             'w_br_nsa': w_br_nsa[l], 'w_br_sb': w_br_sb[l], 'w_br_mem': w_br_mem[l], 'w_out': w_out[l],
             'g_ffn': g_ffn[l], 'w_router': w_router[l], 'b_router': b_router[l],
             'w_up': w_up[l], 'b_up': b_up[l], 'w_down': w_down[l], 'b_down': b_down[l]}
        xp, sp = prompt_layer(xp, mem_prompt, rel_bias, p)
        xs, ss = sample_layer(xs, cache_nsa_cmp[l], cache_nsa_slc[l], cache_sb[l], cache_nsa_win[l],
                              cache_mem[l], page_table, rel_bias, p)
        xp = ffn_block(xp, p, g_final, last=(l == depth - 1))
        xs = ffn_block(xs, p, g_final, last=(l == depth - 1))
        st_p.append(sp)
        st_s.append(ss)
    y_prompt, y_sample = xp, xs

    def stacked(states, i):
        return jnp.stack([s[i] for s in states])

    return (y_prompt, y_sample,
            stacked(st_p, 0), stacked(st_p, 1), stacked(st_p, 2), stacked(st_p, 3), stacked(st_p, 4),
            stacked(st_s, 0), stacked(st_s, 1), stacked(st_s, 2), stacked(st_s, 3))
```

```python
import functools
import math

import jax
import jax.numpy as jnp
import numpy as np
from jax import lax
from jax.experimental import pallas as pl
from jax.experimental.pallas import tpu as pltpu

D_MODEL = 1024
PAGE_SIZE = 128
NSA_HEADS = 8
NSA_KV = 2
NSA_HPG = NSA_HEADS // NSA_KV
HEAD_DIM = 64
CMP_LEN = 32
CMP_STRIDE = 16
CMP_HID = 128
SLC_LEN = 64
SLC_TOP = 16
WINDOW = 512
NSA_QBLK = 32
FORCE_SCORE = 1.0e6
SB_HEADS = 8
SB_QBLK = 128
MEM_HEADS = 4
MEM_HD = 128
REL_BUCKETS = 32
REL_MAX_DIST = 2048
N_EXPERTS = 32
TOP_K = 4
D_FF = 1024
SWIGLU_LIMIT = 7.0
SWIGLU_ALPHA = 1.702
MOE_BLOCK = 256
EPS = 1e-6
NEG_INF = -1.0e30

SEG_QN = NSA_HEADS * HEAD_DIM
SEG_KVN = 6 * NSA_KV * HEAD_DIM
SEG_GN = 3 * NSA_HEADS
SEG_QKVS = 3 * SB_HEADS * HEAD_DIM
SEG_QM = MEM_HEADS * MEM_HD
SEG_GM = 3 * D_MODEL
IN_SIZES = (SEG_QN, SEG_KVN, SEG_GN, SEG_QKVS, SEG_QM, SEG_GM)

VMEM_LIMIT = 48 * 1024 * 1024


def _rms_proj_kernel(x_ref, g_ref, *refs, n_seg, sigmoid):
    w_refs = refs[:n_seg]
    o_refs = refs[n_seg:]
    x = x_ref[...]
    h = x * lax.rsqrt(jnp.mean(x * x, axis=-1, keepdims=True) + EPS) * g_ref[...]
    hb = h.astype(jnp.bfloat16)
    for w_ref, o_ref, sig in zip(w_refs, o_refs, sigmoid):
        z = jnp.dot(hb, w_ref[...], preferred_element_type=jnp.float32)
        if sig:
            z = jax.nn.sigmoid(z)
        o_ref[...] = z.astype(o_ref.dtype)


def rms_proj(x2d, g, weights, sigmoid, out_dtypes, tm):
    n, d = x2d.shape
    assert n % tm == 0
    n_seg = len(weights)
    in_specs = [pl.BlockSpec((tm, d), lambda i: (i, 0)),
                pl.BlockSpec((1, d), lambda i: (0, 0))]
    for w in weights:
        in_specs.append(pl.BlockSpec(w.shape, lambda i: (0, 0), pipeline_mode=pl.Buffered(1)))
    out_specs = [pl.BlockSpec((tm, w.shape[1]), lambda i: (i, 0)) for w in weights]
    out_shape = [jax.ShapeDtypeStruct((n, w.shape[1]), dt) for w, dt in zip(weights, out_dtypes)]
    return pl.pallas_call(
        functools.partial(_rms_proj_kernel, n_seg=n_seg, sigmoid=tuple(sigmoid)),
        grid=(n // tm,),
        in_specs=in_specs,
        out_specs=out_specs,
        out_shape=out_shape,
        compiler_params=pltpu.CompilerParams(dimension_semantics=("parallel",),
                                             vmem_limit_bytes=VMEM_LIMIT),
        name="rms_proj",
    )(x2d, g.reshape(1, d), *weights)


def in_proj(x, g_attn, w_segs):
    B, T, D = x.shape
    n = B * T
    tm = 256 if n % 256 == 0 else n
    outs = rms_proj(x.reshape(n, D), g_attn, w_segs,
                    sigmoid=(False, False, True, False, False, True),
                    out_dtypes=(jnp.float32,) * 6, tm=tm)
    q_n, kv_n, g_n, qkv_s, q_m, g_m = outs
    return (q_n.reshape(B, T, NSA_KV, NSA_HPG, HEAD_DIM),
            kv_n.reshape(B, T, 6, NSA_KV, HEAD_DIM),
            g_n.reshape(B, T, NSA_KV, NSA_HPG, 3),
            qkv_s.reshape(B, T, 3, SB_HEADS, HEAD_DIM),
            q_m.reshape(B, T, MEM_HEADS, MEM_HD),
            g_m.reshape(B, T, 3, D_MODEL))


def split_w_in(w_in):
    offs = np.cumsum((0,) + IN_SIZES)
    return [w_in[:, int(a):int(b)].astype(jnp.bfloat16) for a, b in zip(offs[:-1], offs[1:])]


def _nt_dot(a, b):
    return lax.dot_general(a, b, (((1,), (1,)), ((), ())), preferred_element_type=jnp.float32)


def _online_step(s, mask, v_aug, m, acc, cdt):
    h, tq, tk = s.shape
    s = jnp.where(mask[None], s, NEG_INF)
    m_new = jnp.maximum(m, jnp.max(s, axis=-1, keepdims=True))
    p = jnp.where(mask[None], jnp.exp(s - m_new), 0.0)
    alpha = jnp.exp(m - m_new).reshape(h * tq, 1)
    pv = jnp.dot(p.reshape(h * tq, tk).astype(cdt), v_aug, preferred_element_type=jnp.float32)
    return m_new, alpha * acc + pv


def _finish(acc):
    l = acc[:, HEAD_DIM:HEAD_DIM + 1]
    return acc[:, :HEAD_DIM] / jnp.where(l > 0.0, l, 1.0)


def _nsa_prompt_kernel(q_ref, gate_ref, kc_ref, vc_ref, ks_ref, vs_ref, kw_ref, vw_ref, cb_ref, tb_ref,
                       o_ref, *, tq, n_cmp, ns, cdt):
    f32 = jnp.float32
    qi = pl.program_id(2)
    q0 = qi * tq
    tk = tq
    hpg = NSA_HPG
    rows = hpg * tq
    q = q_ref[0].reshape(rows, HEAD_DIM)

    kc = kc_ref[0, 0]
    ncp = kc.shape[0]
    qpos_c = q0 + lax.broadcasted_iota(jnp.int32, (tq, ncp), 0)
    cidx = lax.broadcasted_iota(jnp.int32, (tq, ncp), 1)
    cmask = (cidx * CMP_STRIDE + (CMP_LEN - 1) <= qpos_c) & (cidx < n_cmp)
    s_c = (_nt_dot(q, kc) + cb_ref[0, 0]).reshape(hpg, tq, ncp)
    s_c = jnp.where(cmask[None], s_c, NEG_INF)
    m_c = jnp.max(s_c, axis=-1, keepdims=True)
    p_c = jnp.where(cmask[None], jnp.exp(s_c - m_c), 0.0)
    l_c = jnp.sum(p_c, axis=-1, keepdims=True)
    p_c = p_c / jnp.where(l_c > 0.0, l_c, 1.0)
    o_c = jnp.dot(p_c.reshape(rows, ncp).astype(cdt), vc_ref[0, 0], preferred_element_type=f32)

    p_sum = p_c[0]
    for h in range(1, hpg):
        p_sum = p_sum + p_c[h]
    jb = lax.broadcasted_iota(jnp.int32, (ns, ncp), 0)
    cc = lax.broadcasted_iota(jnp.int32, (ns, ncp), 1)
    ov_t = ((cc * CMP_STRIDE < jb * SLC_LEN + SLC_LEN)
            & (cc * CMP_STRIDE + (CMP_LEN - 1) >= jb * SLC_LEN)).astype(cdt)
    p_hi = p_sum.astype(cdt)
    p_lo = (p_sum - p_hi.astype(f32)).astype(cdt)
    imp = _nt_dot(ov_t, p_hi) + _nt_dot(ov_t, p_lo)
    blk = lax.broadcasted_iota(jnp.int32, (ns, tq), 0)
    qp_t = q0 + lax.broadcasted_iota(jnp.int32, (ns, tq), 1)
    cur = qp_t // SLC_LEN
    forced = (blk == 0) | (blk == cur) | (blk == cur - 1)
    imp = jnp.where(forced, FORCE_SCORE, imp)
    imp = jnp.where(blk * SLC_LEN <= qp_t, imp, -1.0)
    cnt = jnp.zeros((ns, tq), f32)
    for i in range(ns):
        row = imp[i:i + 1, :]
        ahead = (row > imp) | ((row == imp) & (blk > i))
        cnt = cnt + jnp.where(ahead, 1.0, 0.0)
    sel_t = jnp.where((cnt < float(min(SLC_TOP, ns))) & (imp >= 0.0), 1.0, 0.0)
    sel = sel_t.T.astype(cdt)

    qpos = q0 + lax.broadcasted_iota(jnp.int32, (tq, tk), 0)
    kcol = lax.broadcasted_iota(jnp.int32, (tq, tk), 1)
    eb = lax.broadcasted_iota(jnp.int32, (ns, tk), 0)
    ek = lax.broadcasted_iota(jnp.int32, (ns, tk), 1)
    m0 = jnp.full((hpg, tq, 1), NEG_INF, f32)
    a0 = jnp.zeros((rows, 2 * HEAD_DIM), f32)

    def sel_body(kb, carry):
        m, acc = carry
        k0 = pl.multiple_of(kb * tk, tk)
        s = (_nt_dot(q, ks_ref[0, 0, pl.ds(k0, tk), :]) + tb_ref[0, qi - kb]).reshape(hpg, tq, tk)
        expand = jnp.where(eb == (k0 + ek) // SLC_LEN, 1.0, 0.0).astype(cdt)
        picked = jnp.dot(sel, expand, preferred_element_type=f32)
        mask = (picked > 0.5) & (k0 + kcol <= qpos)
        return _online_step(s, mask, vs_ref[0, 0, pl.ds(k0, tk), :], m, acc, cdt)

    _, acc_s = lax.fori_loop(0, qi + 1, sel_body, (m0, a0))
    o_s = _finish(acc_s)

    def win_body(kb, carry):
        m, acc = carry
        k0 = pl.multiple_of(kb * tk, tk)
        s = (_nt_dot(q, kw_ref[0, 0, pl.ds(k0, tk), :]) + tb_ref[0, qi - kb]).reshape(hpg, tq, tk)
        dist = qpos - (k0 + kcol)
        mask = (dist >= 0) & (dist < WINDOW)
        return _online_step(s, mask, vw_ref[0, 0, pl.ds(k0, tk), :], m, acc, cdt)

    _, acc_w = lax.fori_loop(jnp.maximum(qi - WINDOW // tk, 0), qi + 1, win_body, (m0, a0))
    o_w = _finish(acc_w)

    gates = gate_ref[0, 0]
    outs = []
    for h in range(hpg):
        r = slice(h * tq, (h + 1) * tq)
        outs.append(gates[:, 3 * h:3 * h + 1] * o_c[r]
                    + gates[:, 3 * h + 1:3 * h + 2] * o_s[r]
                    + gates[:, 3 * h + 2:3 * h + 3] * o_w[r])
    o_ref[0] = jnp.concatenate(outs, axis=-1).astype(o_ref.dtype)


def _aug_ones(v):
    return jnp.concatenate([v, jnp.ones_like(v)], axis=-1)


def nsa_bias_tables(rel_bias, t_len, tq):
    nq = t_len // tq
    ncp = t_len // CMP_STRIDE
    G, H = NSA_KV, NSA_HPG
    bias_d = rel_bias[rel_bucket(jnp.arange(t_len, dtype=jnp.int32))]
    bias_d = bias_d.reshape(t_len, G, H).transpose(1, 2, 0)
    q0 = jnp.arange(nq)[:, None] * tq

    def skew(rows, n):
        length = rows.shape[-1]
        lead = rows.shape[:-1]
        flat = jnp.broadcast_to(rows[..., None, :], lead + (n, length)).reshape(lead + (n * length,))
        return flat[..., :n * (length - 1)].reshape(lead + (n, length - 1))

    l = jnp.arange(2 * tq + 1)[None, :]
    w = bias_d[:, :, jnp.clip(q0 + (tq - 1) - l, 0, t_len - 1)]
    tb = skew(w.transpose(0, 2, 1, 3), tq)[..., tq - 1:2 * tq - 1]
    tb = tb.reshape(G, nq, H * tq, tq)
    A = tq // CMP_STRIDE
    b = jnp.arange(CMP_STRIDE)[None, :, None]
    mm = jnp.arange(ncp + A)[None, None, :]
    d_c = q0[:, :, None] + b - ((mm - (A - 1)) * CMP_STRIDE + (CMP_LEN - 1))
    f = bias_d[:, :, jnp.clip(d_c, 0, t_len - 1)]
    cb = skew(f.transpose(0, 2, 1, 3, 4), A)[..., A - 1:A - 1 + ncp]
    cb = cb.transpose(0, 1, 2, 4, 3, 5).reshape(G, nq, H * tq, ncp)
    return tb, cb


def nsa_prompt_pallas(q, gates, kc, vc, k_slc, v_slc, k_win, v_win, rel_bias, *, tq=128, cdt=jnp.bfloat16):
    B, T = q.shape[:2]
    n_cmp = kc.shape[1]
    ncp = T // CMP_STRIDE
    ns = T // SLC_LEN
    nq = T // tq
    rows = NSA_HPG * tq
    tb, cb = nsa_bias_tables(rel_bias, T, tq)
    to_g = lambda a: a.astype(cdt).transpose(0, 2, 1, 3)
    qh = q.astype(cdt).reshape(B, T, NSA_HEADS, HEAD_DIM).transpose(0, 2, 1, 3)
    gt = gates.astype(jnp.float32).reshape(B, T, NSA_KV, NSA_HPG * 3).transpose(0, 2, 1, 3)
    pad_c = ((0, 0), (0, ncp - n_cmp), (0, 0), (0, 0))
    kcg = to_g(jnp.pad(kc, pad_c))
    vcg = to_g(jnp.pad(vc, pad_c))
    ksg, kwg = to_g(k_slc), to_g(k_win)
    vsg, vwg = _aug_ones(to_g(v_slc)), _aug_ones(to_g(v_win))
    per_bg = lambda w: pl.BlockSpec((1, 1, T, w), lambda g, b, i: (b, g, 0, 0))
    per_bg_c = lambda w: pl.BlockSpec((1, 1, ncp, w), lambda g, b, i: (b, g, 0, 0))
    return pl.pallas_call(
        functools.partial(_nsa_prompt_kernel, tq=tq, n_cmp=n_cmp, ns=ns, cdt=cdt),
        grid=(NSA_KV, B, nq),
        in_specs=[
            pl.BlockSpec((1, NSA_HPG, tq, HEAD_DIM), lambda g, b, i: (b, g, i, 0)),
            pl.BlockSpec((1, 1, tq, NSA_HPG * 3), lambda g, b, i: (b, g, i, 0)),
            per_bg_c(HEAD_DIM), per_bg_c(HEAD_DIM),
            per_bg(HEAD_DIM), per_bg(2 * HEAD_DIM), per_bg(HEAD_DIM), per_bg(2 * HEAD_DIM),
            pl.BlockSpec((1, 1, rows, ncp), lambda g, b, i: (g, i, 0, 0)),
            pl.BlockSpec((1, nq, rows, tq), lambda g, b, i: (g, 0, 0, 0), pipeline_mode=pl.Buffered(1)),
        ],
        out_specs=pl.BlockSpec((1, tq, NSA_HPG * HEAD_DIM), lambda g, b, i: (b, i, g)),
        out_shape=jax.ShapeDtypeStruct((B, T, NSA_HEADS * HEAD_DIM), cdt),
        compiler_params=pltpu.CompilerParams(dimension_semantics=("parallel", "parallel", "parallel"),
                                             vmem_limit_bytes=VMEM_LIMIT),
        name="nsa_prompt",
    )(qh, gt, kcg, vcg, ksg, vsg, kwg, vwg, cb, tb)


SB_DEAD_LOG = -104.0


def _split3(x, cdt):
    hi = x.astype(cdt)
    r = x - hi.astype(jnp.float32)
    mid = r.astype(cdt)
    lo = (r - mid.astype(jnp.float32)).astype(cdt)
    return hi, mid, lo


def _sb_tile(z, causal, carry, v, tri, cdt):
    h, tq, tk = z.shape
    sp = jnp.maximum(z, 0.0) + jnp.log1p(jnp.exp(-jnp.abs(z)))
    l1m = jnp.where(causal[None], -sp, 0.0).reshape(h * tq, tk)
    suffix = None
    for part in _split3(l1m, cdt):
        d = jnp.dot(part, tri, preferred_element_type=jnp.float32)
        suffix = d if suffix is None else suffix + d
    surv = (suffix + carry).reshape(h, tq, tk)
    a = jnp.where(causal[None], jnp.exp(z - sp + surv), 0.0)
    pv = jnp.einsum('hqk,hkd->hqd', a.astype(cdt), v, preferred_element_type=jnp.float32)
    return pv, carry + suffix[:, 0:1] + l1m[:, 0:1]


def _sb_prompt_kernel(q_ref, k_ref, v_ref, o_ref, *, tq, hs, cdt):
    f32 = jnp.float32
    qi = pl.program_id(2)
    q0 = qi * tq
    tk = tq
    q = q_ref[0, 0]
    qpos = q0 + lax.broadcasted_iota(jnp.int32, (tq, tk), 0)
    kcol = lax.broadcasted_iota(jnp.int32, (tq, tk), 1)
    tri = jnp.where(lax.broadcasted_iota(jnp.int32, (tk, tk), 0) > lax.broadcasted_iota(jnp.int32, (tk, tk), 1),
                    1.0, 0.0).astype(cdt)

    def cond(state):
        kb, live, _, _ = state
        return (kb >= 0) & (live > SB_DEAD_LOG)

    def body(state):
        kb, _, carry, acc = state
        k0 = pl.multiple_of(kb * tk, tk)
        k = k_ref[0, 0, :, pl.ds(k0, tk), :]
        v = v_ref[0, 0, :, pl.ds(k0, tk), :]
        z = jnp.einsum('hqd,hkd->hqk', q, k, preferred_element_type=f32)
        pv, carry = _sb_tile(z, k0 + kcol < qpos, carry, v, tri, cdt)
        return kb - 1, jnp.max(carry), carry, acc + pv

    init = (qi, jnp.float32(0.0), jnp.zeros((hs * tq, 1), f32), jnp.zeros((hs, tq, HEAD_DIM), f32))
    _, _, _, acc = lax.while_loop(cond, body, init)
    o_ref[0] = jnp.concatenate([acc[h] for h in range(hs)], axis=-1).astype(o_ref.dtype)


def sb_prompt_pallas(qkv, *, tq=128, hs=4, cdt=jnp.bfloat16):
    B, T = qkv.shape[:2]
    x = qkv.astype(cdt).transpose(0, 2, 3, 1, 4)
    full = lambda j: pl.BlockSpec((1, 1, hs, T, HEAD_DIM), lambda b, g, i: (b, j, g, 0, 0))
    return pl.pallas_call(
        functools.partial(_sb_prompt_kernel, tq=tq, hs=hs, cdt=cdt),
        grid=(B, SB_HEADS // hs, T // tq),
        in_specs=[pl.BlockSpec((1, 1, hs, tq, HEAD_DIM), lambda b, g, i: (b, 0, g, i, 0)), full(1), full(2)],
        out_specs=pl.BlockSpec((1, tq, hs * HEAD_DIM), lambda b, g, i: (b, i, g)),
        out_shape=jax.ShapeDtypeStruct((B, T, SB_HEADS * HEAD_DIM), cdt),
        compiler_params=pltpu.CompilerParams(dimension_semantics=("parallel", "parallel", "parallel"),
                                             vmem_limit_bytes=VMEM_LIMIT),
        name="sb_prompt",
    )(x, x, x)


def _moe_kernel(blk_e_ref, n_used_ref, tok_ref, dst_ref, rw_ref, h_hbm, wup_ref, bup_ref, wdn_ref, bdn_ref,
                parts_hbm, xbuf, obuf, sem_in, sem_out, *, blk, n_assign, cdt):
    j = pl.program_id(0)

    def gather_copy(r):
        return pltpu.make_async_copy(h_hbm.at[pl.ds(tok_ref[0, 0, r], 1)], xbuf.at[pl.ds(r, 1)], sem_in)

    def scatter_copy(r):
        return pltpu.make_async_copy(obuf.at[pl.ds(r, 1)], parts_hbm.at[pl.ds(dst_ref[0, 0, r], 1)], sem_out)

    def each_row(fn, real_only):
        def body(r, c):
            if real_only:
                @pl.when(dst_ref[0, 0, r] < n_assign)
                def _():
                    fn(r)
            else:
                fn(r)
            return c
        lax.fori_loop(0, blk, body, 0, unroll=8)

    @pl.when(j < n_used_ref[0])
    def _():
        each_row(lambda r: gather_copy(r).start(), False)
        each_row(lambda r: gather_copy(r).wait(), False)
        x = xbuf[...].astype(cdt)
        gu = jnp.dot(x, wup_ref[0], preferred_element_type=jnp.float32) + bup_ref[0]
        gate = jnp.minimum(gu[:, :D_FF], SWIGLU_LIMIT)
        up = jnp.clip(gu[:, D_FF:], -SWIGLU_LIMIT, SWIGLU_LIMIT)
        act = (up + 1.0) * gate * jax.nn.sigmoid(SWIGLU_ALPHA * gate)
        out = jnp.dot(act.astype(cdt), wdn_ref[0], preferred_element_type=jnp.float32) + bdn_ref[0]
        obuf[...] = out * rw_ref[...]
        each_row(lambda r: scatter_copy(r).start(), True)
        each_row(lambda r: scatter_copy(r).wait(), True)


def moe_ffn_pallas(h_pad, blk_e, n_used, row_tok, row_dst, row_w, w_up, b_up, w_down, b_down, *, blk, n_assign,
                   cdt=jnp.bfloat16):
    n_rows = row_tok.shape[0]
    n_blocks = n_rows // blk
    d = h_pad.shape[1]
    smem_rows = lambda: pl.BlockSpec((1, 1, blk), lambda j, be, nu: (j, 0, 0), memory_space=pltpu.SMEM)
    grid_spec = pltpu.PrefetchScalarGridSpec(
        num_scalar_prefetch=2,
        grid=(n_blocks,),
        in_specs=[
            smem_rows(), smem_rows(),
            pl.BlockSpec((blk, 1), lambda j, be, nu: (j, 0)),
            pl.BlockSpec(memory_space=pl.ANY),
            pl.BlockSpec((1, d, 2 * D_FF), lambda j, be, nu: (be[j], 0, 0)),
            pl.BlockSpec((1, 1, 2 * D_FF), lambda j, be, nu: (be[j], 0, 0)),
            pl.BlockSpec((1, D_FF, d), lambda j, be, nu: (be[j], 0, 0)),
            pl.BlockSpec((1, 1, d), lambda j, be, nu: (be[j], 0, 0)),
        ],
        out_specs=pl.BlockSpec(memory_space=pl.ANY),
        scratch_shapes=[pltpu.VMEM((blk, d), jnp.float32), pltpu.VMEM((blk, d), jnp.float32),
                        pltpu.SemaphoreType.DMA(()), pltpu.SemaphoreType.DMA(())],
    )
    return pl.pallas_call(
        functools.partial(_moe_kernel, blk=blk, n_assign=n_assign, cdt=cdt),
        grid_spec=grid_spec,
        out_shape=jax.ShapeDtypeStruct((n_assign, d), jnp.float32),
        compiler_params=pltpu.CompilerParams(dimension_semantics=("arbitrary",), vmem_limit_bytes=VMEM_LIMIT),
        name="moe_ffn",
    )(blk_e, n_used, row_tok.reshape(n_blocks, 1, blk), row_dst.reshape(n_blocks, 1, blk),
      row_w.reshape(n_rows, 1), h_pad, w_up.astype(cdt), b_up.reshape(N_EXPERTS, 1, 2 * D_FF),
      w_down.astype(cdt), b_down.reshape(N_EXPERTS, 1, d))


def _post_attn_kernel(x_ref, on_ref, os_ref, m_ref, gm_ref, memkv_ref, wn_ref, ws_ref, wm_ref, wo_ref, gf_ref,
                      wr_ref, br_ref, x1_ref, hf_ref, lg_ref, *, with_mem, cdt):
    f32 = jnp.float32
    if with_mem:
        qm = m_ref[0].astype(cdt)
        mem = memkv_ref[0]
        half = MEM_HEADS * MEM_HD
        heads = []
        for h in range(MEM_HEADS):
            c = slice(h * MEM_HD, (h + 1) * MEM_HD)
            s = _nt_dot(qm[:, c], mem[:, c]) * (MEM_HD ** -0.5)
            pm = jnp.exp(s - jnp.max(s, axis=-1, keepdims=True))
            pm = pm / jnp.sum(pm, axis=-1, keepdims=True)
            heads.append(jnp.dot(pm.astype(cdt), mem[:, half + h * MEM_HD:half + (h + 1) * MEM_HD],
                                 preferred_element_type=f32))
        o_m = jnp.concatenate(heads, axis=-1)
    else:
        o_m = m_ref[0]
    d = x_ref.shape[-1]
    gm = gm_ref[0]
    y = (gm[:, :d] * jnp.dot(on_ref[0].astype(cdt), wn_ref[...], preferred_element_type=f32)
         + gm[:, d:2 * d] * jnp.dot(os_ref[0].astype(cdt), ws_ref[...], preferred_element_type=f32)
         + gm[:, 2 * d:] * jnp.dot(o_m.astype(cdt), wm_ref[...], preferred_element_type=f32))
    x1 = x_ref[0] + jnp.dot(y.astype(cdt), wo_ref[...], preferred_element_type=f32)
    x1_ref[0] = x1
    hf = x1 * lax.rsqrt(jnp.mean(x1 * x1, axis=-1, keepdims=True) + EPS) * gf_ref[...]
    hf_ref[0] = hf
    lg_ref[0] = jnp.dot(hf.astype(cdt), wr_ref[...], preferred_element_type=f32) + br_ref[...]


def post_attn_pallas(x, o_n, o_s, m, g_m, mem_kv, p, *, with_mem, cdt=jnp.bfloat16):
    B, T, d = x.shape
    tm = 256 if T % 256 == 0 else T
    if mem_kv is None:
        mem_kv = jnp.zeros((B, 8, 2 * MEM_HEADS * MEM_HD), cdt)
    ml = mem_kv.shape[1]
    rows = lambda w: pl.BlockSpec((1, tm, w), lambda b, i: (b, i, 0))
    const = lambda a: pl.BlockSpec(a.shape, lambda b, i: (0,) * a.ndim)
    weights = [p['w_br_nsa'].astype(cdt), p['w_br_sb'].astype(cdt), p['w_br_mem'].astype(cdt),
               p['w_out'].astype(cdt), p['g_ffn'].reshape(1, d), p['w_router'].astype(cdt),
               p['b_router'].reshape(1, N_EXPERTS)]
    return pl.pallas_call(
        functools.partial(_post_attn_kernel, with_mem=with_mem, cdt=cdt),
        grid=(B, T // tm),
        in_specs=[rows(d), rows(o_n.shape[-1]), rows(o_s.shape[-1]), rows(m.shape[-1]), rows(3 * d),
                  pl.BlockSpec((1, ml, mem_kv.shape[-1]), lambda b, i: (b, 0, 0))] + [const(w) for w in weights],
        out_specs=[rows(d), rows(d), rows(N_EXPERTS)],
        out_shape=[jax.ShapeDtypeStruct((B, T, d), jnp.float32), jax.ShapeDtypeStruct((B, T, d), jnp.float32),
                   jax.ShapeDtypeStruct((B, T, N_EXPERTS), jnp.float32)],
        compiler_params=pltpu.CompilerParams(dimension_semantics=("parallel", "parallel"),
                                             vmem_limit_bytes=VMEM_LIMIT),
        name="post_attn",
    )(x, o_n, o_s, m, g_m, mem_kv.astype(cdt), *weights)


def _combine_norm_kernel(x_ref, parts_ref, g_ref, o_ref, *, normalize):
    x = x_ref[...]
    d = x.shape[1]
    for k in range(TOP_K):
        x = x + parts_ref[:, k * d:(k + 1) * d]
    if normalize:
        x = x * lax.rsqrt(jnp.mean(x * x, axis=-1, keepdims=True) + EPS) * g_ref[...]
    o_ref[...] = x


def combine_norm_pallas(x, parts, g_final, tm, normalize):
    n, d = x.shape
    return pl.pallas_call(
        functools.partial(_combine_norm_kernel, normalize=normalize),
        grid=(n // tm,),
        in_specs=[pl.BlockSpec((tm, d), lambda i: (i, 0)), pl.BlockSpec((tm, TOP_K * d), lambda i: (i, 0)),
                  pl.BlockSpec((1, d), lambda i: (0, 0))],
        out_specs=pl.BlockSpec((tm, d), lambda i: (i, 0)),
        out_shape=jax.ShapeDtypeStruct((n, d), jnp.float32),
        compiler_params=pltpu.CompilerParams(dimension_semantics=("parallel",), vmem_limit_bytes=VMEM_LIMIT),
        name="combine_norm",
    )(x, parts, g_final.reshape(1, d))


def ffn_block(x, hf, logits, p, g_final, last):
    lead = x.shape[:-1]
    x2 = x.reshape(-1, D_MODEL)
    hf = hf.reshape(-1, D_MODEL)
    logits = logits.reshape(-1, N_EXPERTS)
    N = x2.shape[0]
    top_l, top_e = lax.top_k(logits, TOP_K)
    top_w = jax.nn.softmax(top_l, axis=-1)
    n_assign = N * TOP_K
    blk = int(min(MOE_BLOCK, max(8, n_assign // N_EXPERTS)))
    n_rows = (n_assign // blk + N_EXPERTS) * blk
    n_blocks = n_rows // blk
    flat_e = top_e.reshape(-1)
    order = jnp.argsort(flat_e).astype(jnp.int32)
    se = flat_e[order]
    counts = jnp.bincount(flat_e, length=N_EXPERTS)
    start = jnp.cumsum(counts) - counts
    padded = (counts + blk - 1) // blk * blk
    p_end = jnp.cumsum(padded)
    p_start = p_end - padded
    dest = p_start[se] + jnp.arange(n_assign) - start[se]
    row_tok = jnp.full((n_rows,), N, jnp.int32).at[dest].set(order // TOP_K)
    row_dst = jnp.full((n_rows,), n_assign, jnp.int32).at[dest].set(order)
    row_w = jnp.zeros((n_rows,), jnp.float32).at[dest].set(top_w.reshape(-1)[order])
    blk_e = jnp.minimum(jnp.searchsorted(p_end, jnp.arange(n_blocks) * blk, side='right'),
                        N_EXPERTS - 1).astype(jnp.int32)
    n_used = (p_end[-1] // blk).astype(jnp.int32).reshape(1)
    h_pad = jnp.concatenate([hf, jnp.zeros((1, D_MODEL), hf.dtype)], axis=0)
    parts = moe_ffn_pallas(h_pad, blk_e, n_used, row_tok, row_dst, row_w, p['w_up'], p['b_up'],
                           p['w_down'], p['b_down'], blk=blk, n_assign=n_assign)
    tm = 256 if N % 256 == 0 else N
    y = combine_norm_pallas(x2, parts.reshape(N, TOP_K * D_MODEL), g_final, tm, normalize=last)
    return y.reshape(*lead, D_MODEL)


def _rows_matmul_kernel(x_ref, w_ref, o_ref):
    o_ref[...] = jnp.dot(x_ref[...].astype(w_ref.dtype), w_ref[...], preferred_element_type=jnp.float32)


def rows_matmul(x, w, tm):
    r, k = x.shape
    n = w.shape[1]
    return pl.pallas_call(
        _rows_matmul_kernel,
        grid=(r // tm,),
        in_specs=[pl.BlockSpec((tm, k), lambda i: (i, 0)),
                  pl.BlockSpec((k, n), lambda i: (0, 0), pipeline_mode=pl.Buffered(1))],
        out_specs=pl.BlockSpec((tm, n), lambda i: (i, 0)),
        out_shape=jax.ShapeDtypeStruct((r, n), jnp.float32),
        compiler_params=pltpu.CompilerParams(dimension_semantics=("parallel",), vmem_limit_bytes=VMEM_LIMIT),
        name="rows_matmul",
    )(x, w)


def compress_weights(p, cdt):
    R = CMP_LEN // CMP_STRIDE
    nc4 = 2 * NSA_KV
    w1 = jnp.stack([p['cmp_w1_k'], p['cmp_w1_k'], p['cmp_w1_v'], p['cmp_w1_v']])
    w1 = w1.reshape(nc4, R, CMP_STRIDE, HEAD_DIM, CMP_HID)
    eye = jnp.eye(nc4, dtype=w1.dtype)
    wbd = jnp.einsum('crsdh,ce->scdreh', w1, eye).reshape(CMP_STRIDE * nc4 * HEAD_DIM, R * nc4 * CMP_HID)
    w2 = jnp.stack([p['cmp_w2_k'], p['cmp_w2_k'], p['cmp_w2_v'], p['cmp_w2_v']])
    w2bd = jnp.einsum('chd,ce->ched', w2, eye).reshape(nc4 * CMP_HID, nc4 * HEAD_DIM)
    pe = jnp.stack([p['cmp_pe_k'].reshape(-1) @ p['cmp_w1_k'], p['cmp_pe_k'].reshape(-1) @ p['cmp_w1_k'],
                    p['cmp_pe_v'].reshape(-1) @ p['cmp_w1_v'], p['cmp_pe_v'].reshape(-1) @ p['cmp_w1_v']])
    return wbd.astype(cdt), w2bd.astype(cdt), pe.reshape(1, nc4 * CMP_HID)


def _softmax_with_extra(s, mask, s_x):
    s = jnp.where(mask, s, NEG_INF)
    m = jnp.maximum(jnp.max(s, axis=-1, keepdims=True), s_x)
    p = jnp.where(mask, jnp.exp(s - m), 0.0)
    p_x = jnp.exp(s_x - m)
    inv = 1.0 / (jnp.sum(p, axis=-1, keepdims=True) + p_x)
    return p * inv, p_x * inv


def _nsa_decode_kernel(pt_ref, qk_ref, gate_ref, news_ref, neww_ref, win_ref, pe_ref, w2_ref, cbias_ref,
                       sbias_ref, wbias_ref, b0_ref, e_ref, h_hbm, slc_hbm, o_ref, hbuf, sbuf, sem,
                       *, n_pages, n_cmp, ns, cdt):
    f32 = jnp.float32
    b = pl.program_id(0)
    nb = pl.num_programs(0)
    slot = b % 2
    hpp = PAGE_SIZE // CMP_STRIDE

    def copies(bb, sl):
        out = []
        for p in range(n_pages):
            pg = pt_ref[bb, p]
            out.append(pltpu.make_async_copy(h_hbm.at[pg], hbuf.at[sl, pl.ds(p * hpp, hpp)], sem.at[0, sl]))
            out.append(pltpu.make_async_copy(slc_hbm.at[pg], sbuf.at[sl, pl.ds(p * PAGE_SIZE, PAGE_SIZE)],
                                             sem.at[1, sl]))
        return out

    @pl.when(b == 0)
    def _():
        for c in copies(0, 0):
            c.start()

    @pl.when(b + 1 < nb)
    def _():
        for c in copies(b + 1, 1 - slot):
            c.start()

    for c in copies(b, slot):
        c.wait()

    qk = qk_ref[0]
    nh = qk.shape[0]
    width = qk.shape[1]
    row = lax.broadcasted_iota(jnp.int32, (nh, 1), 0)

    def own_values(o_full):
        v0 = o_full[:, NSA_KV * HEAD_DIM:NSA_KV * HEAD_DIM + HEAD_DIM]
        v1 = o_full[:, NSA_KV * HEAD_DIM + HEAD_DIM:]
        return jnp.where(row < NSA_HPG, v0, v1)

    hh = hbuf[slot]
    ncp = hh.shape[0]
    half = hh.shape[1] // 2
    pre = hh[:, :half] + pltpu.roll(hh[:, half:], ncp - 1, 0) + pe_ref[...]
    kvc = jnp.dot(jax.nn.gelu(pre).astype(cdt), w2_ref[...], preferred_element_type=f32).astype(cdt)
    cidx = lax.broadcasted_iota(jnp.int32, (nh, ncp), 1)
    cmask = cidx < n_cmp
    s_c = jnp.where(cmask, _nt_dot(qk, kvc) + cbias_ref[...], NEG_INF)
    m_c = jnp.max(s_c, axis=-1, keepdims=True)
    p_c = jnp.where(cmask, jnp.exp(s_c - m_c), 0.0)
    p_c = p_c / jnp.sum(p_c, axis=-1, keepdims=True)
    o_c = own_values(jnp.dot(p_c.astype(cdt), kvc, preferred_element_type=f32))

    sum0 = jnp.sum(jnp.where(row < NSA_HPG, p_c, 0.0), axis=0, keepdims=True)
    sum1 = jnp.sum(jnp.where(row < NSA_HPG, 0.0, p_c), axis=0, keepdims=True)
    p_sum = jnp.where(row < NSA_HPG, sum0, sum1)
    nsp = e_ref.shape[0]
    cc = lax.broadcasted_iota(jnp.int32, (ncp, nsp), 0)
    jb = lax.broadcasted_iota(jnp.int32, (ncp, nsp), 1)
    ov = ((cc * CMP_STRIDE < jb * SLC_LEN + SLC_LEN)
          & (cc * CMP_STRIDE + (CMP_LEN - 1) >= jb * SLC_LEN)).astype(cdt)
    p_hi = p_sum.astype(cdt)
    p_lo = (p_sum - p_hi.astype(f32)).astype(cdt)
    imp = jnp.dot(p_hi, ov, preferred_element_type=f32) + jnp.dot(p_lo, ov, preferred_element_type=f32)
    past = n_pages * PAGE_SIZE
    blk = lax.broadcasted_iota(jnp.int32, (nh, nsp), 1)
    cur = past // SLC_LEN
    forced = (blk == 0) | (blk == cur) | (blk == cur - 1)
    imp = jnp.where(forced, FORCE_SCORE, imp)
    imp = jnp.where(blk < ns, imp, -1.0)
    imp_t = imp.T
    bi = lax.broadcasted_iota(jnp.int32, (nsp, nsp), 0)
    bj = lax.broadcasted_iota(jnp.int32, (nsp, nsp), 1)
    sel_rows = []
    for g in range(NSA_KV):
        col = imp_t[:, g * NSA_HPG:g * NSA_HPG + 1]
        rw = imp[g * NSA_HPG:g * NSA_HPG + 1, :]
        ahead = (col > rw) | ((col == rw) & (bi < bj))
        cnt = jnp.sum(jnp.where(ahead, 1.0, 0.0), axis=0, keepdims=True)
        sel_rows.append(jnp.where((cnt < float(min(SLC_TOP, ns))) & (rw >= 0.0), 1.0, 0.0))
    sel = jnp.where(row < NSA_HPG, sel_rows[0], sel_rows[1])

    keys = sbuf[slot].astype(cdt)
    picked = jnp.dot(sel.astype(cdt), e_ref[...], preferred_element_type=f32)
    new_s = news_ref[0]
    s_x = jnp.sum(qk.astype(f32) * new_s.astype(cdt).astype(f32), axis=-1, keepdims=True) + b0_ref[...]
    p_s, p_x = _softmax_with_extra(_nt_dot(qk, keys) + sbias_ref[...], picked > 0.5, s_x)
    o_s = own_values(jnp.dot(p_s.astype(cdt), keys, preferred_element_type=f32)
                     + p_x.astype(cdt).astype(f32) * new_s.astype(cdt).astype(f32))

    wrows = win_ref[0].astype(cdt)
    wb = wrows.shape[0]
    new_w = neww_ref[0]
    widx = lax.broadcasted_iota(jnp.int32, (nh, wb), 1)
    s_xw = jnp.sum(qk.astype(f32) * new_w.astype(cdt).astype(f32), axis=-1, keepdims=True) + b0_ref[...]
    p_w, p_xw = _softmax_with_extra(_nt_dot(qk, wrows) + wbias_ref[...], wb - widx < WINDOW, s_xw)
    o_w = own_values(jnp.dot(p_w.astype(cdt), wrows, preferred_element_type=f32)
                     + p_xw.astype(cdt).astype(f32) * new_w.astype(cdt).astype(f32))

    gates = gate_ref[0]
    o_ref[0] = gates[:, 0:1] * o_c + gates[:, 1:2] * o_s + gates[:, 2:3] * o_w


def nsa_decode_pallas(q_n, gates, kv_new, c_win, h_all, c_slc, page_table, rel_bias, w2bd, pe_term,
                      cdt=jnp.bfloat16):
    B = q_n.shape[0]
    n_pages = page_table.shape[1]
    past = n_pages * PAGE_SIZE
    wb = c_win.shape[1]
    hpp = PAGE_SIZE // CMP_STRIDE
    ncp = n_pages * hpp
    n_cmp = ncp - (CMP_LEN // CMP_STRIDE - 1)
    ns = -(-(past + 1) // SLC_LEN)
    nsp = -(-ns // 128) * 128
    width = 2 * NSA_KV * HEAD_DIM
    H = NSA_HEADS
    qs = (q_n * HEAD_DIM ** -0.5).reshape(B, NSA_KV, NSA_HPG, HEAD_DIM)
    qk = jnp.einsum('bghd,ge->bghed', qs, jnp.eye(NSA_KV, dtype=qs.dtype)).reshape(B, H, NSA_KV * HEAD_DIM)
    qk = jnp.pad(qk, ((0, 0), (0, 0), (0, width - NSA_KV * HEAD_DIM))).astype(cdt)
    bias_d = rel_bias[rel_bucket(jnp.arange(past + 1, dtype=jnp.int32))].T
    c_end = jnp.arange(ncp) * CMP_STRIDE + (CMP_LEN - 1)
    cbias = bias_d[:, jnp.clip(past - c_end, 0, past)]
    sbias = bias_d[:, past - jnp.arange(past)]
    wbias = bias_d[:, jnp.clip(wb - jnp.arange(wb), 0, past)]
    b0 = bias_d[:, 0:1]
    expand = (jnp.arange(nsp)[:, None] == (jnp.arange(past) // SLC_LEN)[None, :]).astype(cdt)
    const = lambda shape: pl.BlockSpec(shape, lambda b, pt: (0,) * len(shape))
    per_b = lambda shape: pl.BlockSpec((1,) + shape, lambda b, pt: (b,) + (0,) * len(shape))
    grid_spec = pltpu.PrefetchScalarGridSpec(
        num_scalar_prefetch=1,
        grid=(B,),
        in_specs=[per_b((H, width)), per_b((H, 3)), per_b((1, width)), per_b((1, width)), per_b((wb, width)),
                  const((1, pe_term.shape[1])), const(w2bd.shape), const((H, ncp)), const((H, past)),
                  const((H, wb)), const((H, 1)),
                  pl.BlockSpec((nsp, past), lambda b, pt: (0, 0), pipeline_mode=pl.Buffered(1)),
                  pl.BlockSpec(memory_space=pl.ANY), pl.BlockSpec(memory_space=pl.ANY)],
        out_specs=per_b((H, HEAD_DIM)),
        scratch_shapes=[pltpu.VMEM((2, ncp, h_all.shape[2]), jnp.float32),
                        pltpu.VMEM((2, past, width), jnp.float32),
                        pltpu.SemaphoreType.DMA((2, 2))],
    )
    out = pl.pallas_call(
        functools.partial(_nsa_decode_kernel, n_pages=n_pages, n_cmp=n_cmp, ns=ns, cdt=cdt),
        grid_spec=grid_spec,
        out_shape=jax.ShapeDtypeStruct((B, H, HEAD_DIM), jnp.float32),
        compiler_params=pltpu.CompilerParams(dimension_semantics=("arbitrary",), vmem_limit_bytes=VMEM_LIMIT),
        name="nsa_decode",
    )(page_table, qk, gates.reshape(B, H, 3), kv_new[:, 2:4].reshape(B, 1, width),
      kv_new[:, 4:6].reshape(B, 1, width), c_win.reshape(B, wb, width), pe_term, w2bd, cbias, sbias, wbias, b0,
      expand, h_all, c_slc.reshape(c_slc.shape[0], PAGE_SIZE, width))
    return out.reshape(B, H * HEAD_DIM)


def _compress_finish_kernel(h_ref, pe_ref, w2_ref, o_ref):
    hh = h_ref[0]
    nh = hh.shape[0]
    half = hh.shape[1] // 2
    pre = hh[:, :half] + pltpu.roll(hh[:, half:], nh - 1, 0) + pe_ref[...]
    o_ref[0] = jnp.dot(jax.nn.gelu(pre).astype(w2_ref.dtype), w2_ref[...], preferred_element_type=jnp.float32)


def compress_prompt_pallas(cmp_rows, wbd, w2bd, pe_term):
    B, T = cmp_rows.shape[:2]
    nh = T // CMP_STRIDE
    width = 2 * NSA_KV * HEAD_DIM
    rows = B * nh
    tm = next(t for t in (512, 256, 128, 64, 32, 16, 8) if rows % t == 0)
    h = rows_matmul(cmp_rows.reshape(rows, CMP_STRIDE * width), wbd, tm).reshape(B, nh, wbd.shape[1])
    out = pl.pallas_call(
        _compress_finish_kernel,
        grid=(B,),
        in_specs=[pl.BlockSpec((1, nh, wbd.shape[1]), lambda b: (b, 0, 0)),
                  pl.BlockSpec(pe_term.shape, lambda b: (0, 0)),
                  pl.BlockSpec(w2bd.shape, lambda b: (0, 0))],
        out_specs=pl.BlockSpec((1, nh, width), lambda b: (b, 0, 0)),
        out_shape=jax.ShapeDtypeStruct((B, nh, width), jnp.float32),
        compiler_params=pltpu.CompilerParams(dimension_semantics=("parallel",), vmem_limit_bytes=VMEM_LIMIT),
        name="compress_finish",
    )(h, pe_term, w2bd)
    return out.reshape(B, nh, 2, NSA_KV, HEAD_DIM)


SB_CHUNK_PAGES = 2


def _sb_mem_decode_kernel(pt_ref, qs_ref, qm_ref, mem_ref, sb_hbm, os_ref, om_ref, first, extra, sem,
                          *, n_pages, cdt):
    f32 = jnp.float32
    b = pl.program_id(0)
    nb = pl.num_programs(0)
    slot = b % 2
    ch = SB_CHUNK_PAGES
    ck = ch * PAGE_SIZE
    n_chunks = n_pages // ch

    def copies(bb, c, dst, s):
        return [pltpu.make_async_copy(sb_hbm.at[pt_ref[bb, n_pages - (c + 1) * ch + i]],
                                      dst.at[pl.ds(i * PAGE_SIZE, PAGE_SIZE)], s) for i in range(ch)]

    @pl.when(b == 0)
    def _():
        for c in copies(0, 0, first.at[0], sem.at[0]):
            c.start()

    @pl.when(b + 1 < nb)
    def _():
        for c in copies(b + 1, 0, first.at[1 - slot], sem.at[1 - slot]):
            c.start()

    for c in copies(b, 0, first.at[slot], sem.at[slot]):
        c.wait()

    qs = qs_ref[0]
    nh = qs.shape[0]
    tri = jnp.where(lax.broadcasted_iota(jnp.int32, (ck, ck), 0) > lax.broadcasted_iota(jnp.int32, (ck, ck), 1),
                    1.0, 0.0).astype(cdt)

    def chunk(rows, carry, acc):
        keys = rows.astype(cdt)
        z = _nt_dot(qs, keys)
        sp = jnp.maximum(z, 0.0) + jnp.log1p(jnp.exp(-jnp.abs(z)))
        l1m = -sp
        suffix = None
        for part in _split3(l1m, cdt):
            d = jnp.dot(part, tri, preferred_element_type=f32)
            suffix = d if suffix is None else suffix + d
        a = jnp.exp(z - sp + suffix + carry)
        acc = acc + jnp.dot(a.astype(cdt), keys, preferred_element_type=f32)
        return carry + suffix[:, 0:1] + l1m[:, 0:1], acc

    carry, acc = chunk(first[slot], jnp.zeros((nh, 1), f32), jnp.zeros((nh, qs.shape[1]), f32))

    def cond(state):
        c, live, _, _ = state
        return (c < n_chunks) & (live > SB_DEAD_LOG)

    def body(state):
        c, _, carry, acc = state
        for cp in copies(b, c, extra, sem.at[2]):
            cp.start()
        for cp in copies(b, c, extra, sem.at[2]):
            cp.wait()
        carry, acc = chunk(extra[...], carry, acc)
        return c + 1, jnp.max(carry), carry, acc

    _, _, _, acc = lax.while_loop(cond, body, (jnp.int32(1), jnp.max(carry), carry, acc))
    v0 = SB_HEADS * HEAD_DIM
    os_ref[0] = jnp.concatenate([acc[h:h + 1, v0 + h * HEAD_DIM:v0 + (h + 1) * HEAD_DIM] for h in range(nh)], axis=0)

    mem = mem_ref[0].astype(cdt)
    s = _nt_dot(qm_ref[0], mem) * (MEM_HD ** -0.5)
    p = jnp.exp(s - jnp.max(s, axis=-1, keepdims=True))
    p = p / jnp.sum(p, axis=-1, keepdims=True)
    om = jnp.dot(p.astype(cdt), mem, preferred_element_type=f32)
    m0 = MEM_HEADS * MEM_HD
    om_ref[0] = jnp.concatenate([om[h:h + 1, m0 + h * MEM_HD:m0 + (h + 1) * MEM_HD] for h in range(MEM_HEADS)], axis=0)


def sb_mem_decode_pallas(q_s, q_m, c_sb, c_mem, page_table, cdt=jnp.bfloat16):
    B = q_s.shape[0]
    n_pages = page_table.shape[1]
    ws = 2 * SB_HEADS * HEAD_DIM
    wm = 2 * MEM_HEADS * MEM_HD
    ml = c_mem.shape[1]
    qs = jnp.einsum('bhd,he->bhed', q_s * HEAD_DIM ** -0.5, jnp.eye(SB_HEADS, dtype=q_s.dtype))
    qs = jnp.pad(qs.reshape(B, SB_HEADS, ws // 2), ((0, 0), (0, 0), (0, ws // 2))).astype(cdt)
    qm = jnp.einsum('bhd,he->bhed', q_m, jnp.eye(MEM_HEADS, dtype=q_m.dtype)).reshape(B, MEM_HEADS, wm // 2)
    qm = jnp.pad(qm, ((0, 0), (0, 8 - MEM_HEADS), (0, wm // 2))).astype(cdt)
    per_b = lambda shape: pl.BlockSpec((1,) + shape, lambda b, pt: (b,) + (0,) * len(shape))
    ck = SB_CHUNK_PAGES * PAGE_SIZE
    grid_spec = pltpu.PrefetchScalarGridSpec(
        num_scalar_prefetch=1,
        grid=(B,),
        in_specs=[per_b((SB_HEADS, ws)), per_b((8, wm)), per_b((ml, wm)), pl.BlockSpec(memory_space=pl.ANY)],
        out_specs=[per_b((SB_HEADS, HEAD_DIM)), per_b((MEM_HEADS, MEM_HD))],
        scratch_shapes=[pltpu.VMEM((2, ck, ws), jnp.float32), pltpu.VMEM((ck, ws), jnp.float32),
                        pltpu.SemaphoreType.DMA((3,))],
    )
    o_s, o_m = pl.pallas_call(
        functools.partial(_sb_mem_decode_kernel, n_pages=n_pages, cdt=cdt),
        grid_spec=grid_spec,
        out_shape=[jax.ShapeDtypeStruct((B, SB_HEADS, HEAD_DIM), jnp.float32),
                   jax.ShapeDtypeStruct((B, MEM_HEADS, MEM_HD), jnp.float32)],
        compiler_params=pltpu.CompilerParams(dimension_semantics=("arbitrary",), vmem_limit_bytes=VMEM_LIMIT),
        name="sb_mem_decode",
    )(page_table, qs, qm, c_mem.reshape(B, ml, wm), c_sb.reshape(c_sb.shape[0], PAGE_SIZE, ws))
    return o_s.reshape(B, SB_HEADS * HEAD_DIM), o_m.reshape(B, MEM_HEADS * MEM_HD)


def rmsnorm(x, g):
    xf = x.astype(jnp.float32)
    xf = xf * lax.rsqrt(jnp.mean(xf * xf, axis=-1, keepdims=True) + EPS)
    return (xf * g.astype(jnp.float32)).astype(x.dtype)


def masked_softmax(s, mask):
    p = jax.nn.softmax(jnp.where(mask, s, NEG_INF), axis=-1)
    return jnp.where(mask, p, 0.0)


def rel_bucket(dist):
    n = jnp.maximum(dist, 0)
    max_exact = REL_BUCKETS // 2
    nf = jnp.maximum(n, 1).astype(jnp.float32)
    large = max_exact + (jnp.log(nf / max_exact) / math.log(REL_MAX_DIST / max_exact)
                         * (REL_BUCKETS - max_exact)).astype(jnp.int32)
    return jnp.where(n < max_exact, n, jnp.minimum(large, REL_BUCKETS - 1))


def compress(x, pe, w1, w2):
    B, T = x.shape[:2]
    R = CMP_LEN // CMP_STRIDE
    nh = T // CMP_STRIDE
    nc = nh - R + 1
    halves = x[:, :nh * CMP_STRIDE].reshape(B, nh, CMP_STRIDE, NSA_KV, HEAD_DIM)
    w1h = w1.reshape(R, CMP_STRIDE, HEAD_DIM, CMP_HID)
    pre = pe.reshape(-1) @ w1
    for r in range(R):
        pre = pre + jnp.einsum('bnsgd,sdh->bngh', halves[:, r:r + nc], w1h[r])
    return jax.nn.gelu(pre) @ w2


def to_blocks(x):
    B, T = x.shape[:2]
    ns = -(-T // SLC_LEN)
    x = jnp.pad(x, ((0, 0), (0, ns * SLC_LEN - T), (0, 0), (0, 0)))
    return x.reshape(B, ns, SLC_LEN, NSA_KV, HEAD_DIM).transpose(0, 3, 1, 2, 4)


def nsa_attend(q, q_pos, gates, kc, vc, ks_blk, vs_blk, kw, vw, kw_pos, rel_bias):
    B, Q = q.shape[:2]
    NC = kc.shape[1]
    NS = ks_blk.shape[2]
    KW = kw.shape[1]
    scale = HEAD_DIM ** -0.5
    f32 = jnp.float32
    c_start = jnp.arange(NC, dtype=jnp.int32) * CMP_STRIDE
    c_end = c_start + (CMP_LEN - 1)
    c_mask = (c_end[None, :] <= q_pos[:, None])[:, None, None, :]
    c_bias = rel_bias[rel_bucket(q_pos[:, None] - c_end[None, :])].reshape(
        Q, NC, NSA_KV, NSA_HPG).transpose(0, 2, 3, 1)
    s_c = jnp.einsum('bqghd,bcgd->bqghc', q, kc, preferred_element_type=f32) * scale + c_bias
    p_c = masked_softmax(s_c, c_mask)
    o_c = jnp.einsum('bqghc,bcgd->bqghd', p_c.astype(vc.dtype), vc)
    s_start = jnp.arange(NS, dtype=jnp.int32) * SLC_LEN
    overlap = ((c_start[:, None] < s_start[None, :] + SLC_LEN)
               & (c_end[:, None] >= s_start[None, :])).astype(f32)
    imp = jnp.einsum('bqghc,cj->bqgj', p_c, overlap)
    blk = jnp.arange(NS, dtype=jnp.int32)[None, :]
    cur = (q_pos // SLC_LEN)[:, None]
    forced = (blk == 0) | (blk == cur) | (blk == cur - 1)
    valid = s_start[None, :] <= q_pos[:, None]
    imp = jnp.where(forced[None, :, None, :], FORCE_SCORE, imp)
    imp = jnp.where(valid[None, :, None, :], imp, -1.0)
    n_top = min(SLC_TOP, NS)
    top_val, idx = lax.top_k(imp, n_top)
    bi = jnp.arange(B)[:, None, None, None]
    gi = jnp.arange(NSA_KV)[None, None, :, None]
    k_sel = ks_blk[bi, gi, idx]
    v_sel = vs_blk[bi, gi, idx]
    k_pos = idx[..., None] * SLC_LEN + jnp.arange(SLC_LEN, dtype=jnp.int32)
    qp = q_pos[None, :, None, None, None]
    s_mask = (top_val >= 0)[..., None] & (k_pos <= qp)
    bias_g = rel_bias.reshape(REL_BUCKETS, NSA_KV, NSA_HPG).transpose(1, 0, 2)
    s_bias = bias_g[gi[..., None], rel_bucket(qp - k_pos)]
    n_keys = n_top * SLC_LEN
    s_bias = s_bias.transpose(0, 1, 2, 5, 3, 4).reshape(B, Q, NSA_KV, NSA_HPG, n_keys)
    s_s = jnp.einsum('bqghd,bqgnkd->bqghnk', q, k_sel, preferred_element_type=f32).reshape(
        B, Q, NSA_KV, NSA_HPG, n_keys) * scale + s_bias
    p_s = masked_softmax(s_s, s_mask.reshape(B, Q, NSA_KV, 1, n_keys))
    o_s = jnp.einsum('bqghk,bqgkd->bqghd', p_s.astype(v_sel.dtype),
                     v_sel.reshape(B, Q, NSA_KV, n_keys, HEAD_DIM))
    w_dist = q_pos[:, None] - kw_pos[None, :]
    w_mask = ((w_dist >= 0) & (w_dist < WINDOW) & (kw_pos[None, :] >= 0))[:, None, None, :]
    w_bias = rel_bias[rel_bucket(w_dist)].reshape(Q, KW, NSA_KV, NSA_HPG).transpose(0, 2, 3, 1)
    s_w = jnp.einsum('bqghd,bkgd->bqghk', q, kw, preferred_element_type=f32) * scale + w_bias
    p_w = masked_softmax(s_w, w_mask)
    o_w = jnp.einsum('bqghk,bkgd->bqghd', p_w.astype(vw.dtype), vw)
    g = gates.astype(o_c.dtype)
    o = g[..., 0:1] * o_c + g[..., 1:2] * o_s + g[..., 2:3] * o_w
    return o.reshape(B, Q, NSA_HEADS * HEAD_DIM)


def nsa_prompt(q, gates, kc, vc, ks_blk, vs_blk, k_win, v_win, rel_bias):
    B, T = q.shape[:2]
    pad = ((0, 0), (WINDOW, 0), (0, 0), (0, 0))
    kw_pad = jnp.pad(k_win, pad)
    vw_pad = jnp.pad(v_win, pad)

    def block(i):
        t0 = i * NSA_QBLK
        qb = lax.dynamic_slice_in_dim(q, t0, NSA_QBLK, axis=1)
        gb = lax.dynamic_slice_in_dim(gates, t0, NSA_QBLK, axis=1)
        kw = lax.dynamic_slice_in_dim(kw_pad, t0, WINDOW + NSA_QBLK, axis=1)
        vw = lax.dynamic_slice_in_dim(vw_pad, t0, WINDOW + NSA_QBLK, axis=1)
        q_pos = t0 + jnp.arange(NSA_QBLK, dtype=jnp.int32)
        kw_pos = t0 - WINDOW + jnp.arange(WINDOW + NSA_QBLK, dtype=jnp.int32)
        return nsa_attend(qb, q_pos, gb, kc, vc, ks_blk, vs_blk, kw, vw, kw_pos, rel_bias)

    out = lax.map(block, jnp.arange(T // NSA_QBLK, dtype=jnp.int32))
    return out.transpose(1, 0, 2, 3).reshape(B, T, NSA_HEADS * HEAD_DIM)


def sb_attend(q, q_pos, k, v, k_pos):
    z = jnp.einsum('bqhd,bkhd->bhqk', q, k, preferred_element_type=jnp.float32) * (HEAD_DIM ** -0.5)
    causal = k_pos[None, :] < q_pos[:, None]
    log_1m = jnp.where(causal, -jax.nn.softplus(z), 0.0)
    surv = lax.cumsum(log_1m, axis=3, reverse=True) - log_1m
    a = jnp.where(causal, jnp.exp(jax.nn.log_sigmoid(z) + surv), 0.0)
    return jnp.einsum('bhqk,bkhd->bqhd', a.astype(v.dtype), v)


def sb_prompt(q, k, v):
    B, T = q.shape[:2]
    k_pos = jnp.arange(T, dtype=jnp.int32)

    def block(i):
        t0 = i * SB_QBLK
        qb = lax.dynamic_slice_in_dim(q, t0, SB_QBLK, axis=1)
        return sb_attend(qb, t0 + jnp.arange(SB_QBLK, dtype=jnp.int32), k, v, k_pos)

    out = lax.map(block, jnp.arange(T // SB_QBLK, dtype=jnp.int32))
    return out.transpose(1, 0, 2, 3, 4).reshape(B, T, SB_HEADS * HEAD_DIM)


def mem_attend(q, mk, mv):
    s = jnp.einsum('bqhd,bmhd->bhqm', q, mk, preferred_element_type=jnp.float32) * (MEM_HD ** -0.5)
    p = jax.nn.softmax(s, axis=-1)
    return jnp.einsum('bhqm,bmhd->bqhd', p.astype(mv.dtype), mv)


def merge_out(o_n, o_s, o_m, g_m, p):
    y = (g_m[:, :, 0] * (o_n @ p['w_br_nsa'])
         + g_m[:, :, 1] * (o_s @ p['w_br_sb'])
         + g_m[:, :, 2] * (o_m @ p['w_br_mem']))
    return y @ p['w_out']


def moe(h, w_router, b_router, w_up, b_up, w_down, b_down):
    lead = h.shape[:-1]
    xf = h.reshape(-1, D_MODEL)
    N = xf.shape[0]
    logits = jnp.einsum('nd,de->ne', xf, w_router, preferred_element_type=jnp.float32) + b_router.astype(jnp.float32)
    top_l, top_e = lax.top_k(logits, TOP_K)
    top_w = jax.nn.softmax(top_l, axis=-1)
    n_assign = N * TOP_K
    blk = int(min(MOE_BLOCK, max(8, n_assign // N_EXPERTS)))
    n_rows = (n_assign // blk + N_EXPERTS) * blk
    n_blocks = n_rows // blk
    flat_e = top_e.reshape(-1)
    flat_t = jnp.arange(n_assign, dtype=jnp.int32) // TOP_K
    flat_w = top_w.reshape(-1)
    order = jnp.argsort(flat_e)
    se = flat_e[order]
    counts = jnp.bincount(flat_e, length=N_EXPERTS)
    start = jnp.cumsum(counts) - counts
    padded = (counts + blk - 1) // blk * blk
    p_end = jnp.cumsum(padded)
    p_start = p_end - padded
    dest = p_start[se] + jnp.arange(n_assign) - start[se]
    row_tok = jnp.full((n_rows,), N, jnp.int32).at[dest].set(flat_t[order])
    row_w = jnp.zeros((n_rows,), jnp.float32).at[dest].set(flat_w[order])
    blk_e = jnp.minimum(jnp.searchsorted(p_end, jnp.arange(n_blocks) * blk, side='right'), N_EXPERTS - 1)
    x_pad = jnp.concatenate([xf, jnp.zeros((1, D_MODEL), xf.dtype)], axis=0)

    def run(args):
        e, toks = args
        gu = x_pad[toks] @ w_up[e] + b_up[e]
        gate = jnp.minimum(gu[:, :D_FF], SWIGLU_LIMIT)
        up = jnp.clip(gu[:, D_FF:], -SWIGLU_LIMIT, SWIGLU_LIMIT)
        act = (up + 1.0) * gate * jax.nn.sigmoid(SWIGLU_ALPHA * gate)
        return act @ w_down[e] + b_down[e]

    out = lax.map(run, (blk_e, row_tok.reshape(n_blocks, blk)))
    out = out.reshape(n_rows, D_MODEL) * row_w[:, None].astype(out.dtype)
    y = jnp.zeros((N + 1, D_MODEL), out.dtype).at[row_tok].add(out)[:N]
    return y.reshape(*lead, D_MODEL)


def ffn_sublayer(x, p):
    return x + moe(rmsnorm(x, p['g_ffn']), p['w_router'], p['b_router'],
                   p['w_up'], p['b_up'], p['w_down'], p['b_down'])


def prompt_layer(x, mem, rel_bias, p):
    B, T = x.shape[:2]
    q_n, kv_n, g_n, qkv_s, q_m, g_m = in_proj(x, p['g_attn'], p['w_in_segs'])
    n_cmp = T // CMP_STRIDE - (CMP_LEN // CMP_STRIDE - 1)
    kvc = compress_prompt_pallas(kv_n[:, :, 0:2], p['cmp_wbd'], p['cmp_w2bd'], p['cmp_pe_term'])[:, :n_cmp]
    att_scale = HEAD_DIM ** -0.5
    o_n = nsa_prompt_pallas(q_n * att_scale, g_n, kvc[:, :, 0], kvc[:, :, 1], kv_n[:, :, 2], kv_n[:, :, 3],
                            kv_n[:, :, 4], kv_n[:, :, 5], rel_bias)
    o_s = sb_prompt_pallas(qkv_s * jnp.array([att_scale, 1.0, 1.0], jnp.float32).reshape(1, 1, 3, 1, 1))
    ml = mem.shape[1]
    mem_rows = B * ml
    tm_mem = 256 if mem_rows % 256 == 0 else mem_rows
    mem_kv = rms_proj(mem.reshape(mem_rows, D_MODEL), p['g_mem'], [p['w_mem_kv'].astype(jnp.bfloat16)],
                      sigmoid=(False,), out_dtypes=(jnp.float32,), tm=tm_mem)[0].reshape(B, ml, -1)
    post = post_attn_pallas(x, o_n, o_s, q_m.reshape(B, T, -1), g_m.reshape(B, T, -1), mem_kv, p, with_mem=True)
    wb = min(WINDOW, T)
    return post, (kv_n[:, :, 0:2], kv_n[:, :, 2:4], qkv_s[:, :, 1:3], kv_n[:, T - wb:, 4:6],
                  mem_kv.reshape(B, ml, 2, MEM_HEADS, MEM_HD))


def sample_layer(x, c_cmp, c_slc, c_sb, c_win, c_mem, page_table, rel_bias, p):
    B, S = x.shape[:2]
    assert S == 1, "the decode kernels handle one new token per sequence"
    q_n, kv_n, g_n, qkv_s, q_m, g_m = in_proj(x, p['g_attn'], p['w_in_segs'])
    win_buf = jnp.concatenate([c_win, kv_n[:, :, 4:6]], axis=1)
    n_phys = c_cmp.shape[0]
    hpp = PAGE_SIZE // CMP_STRIDE
    rows = n_phys * hpp
    tm = next(t for t in (512, 256, 128, 64, 32, 16, 8) if rows % t == 0)
    h_all = rows_matmul(c_cmp.reshape(rows, -1), p['cmp_wbd'], tm).reshape(n_phys, hpp, -1)
    o_n = nsa_decode_pallas(q_n[:, 0], g_n[:, 0], kv_n[:, 0], c_win, h_all, c_slc, page_table, rel_bias,
                            p['cmp_w2bd'], p['cmp_pe_term'])
    o_s, o_m = sb_mem_decode_pallas(qkv_s[:, 0, 0], q_m[:, 0], c_sb, c_mem, page_table)
    post = post_attn_pallas(x.reshape(1, B, D_MODEL), o_n[None], o_s[None], o_m[None], g_m.reshape(1, B, -1),
                            None, p, with_mem=False)
    post = [a.reshape(B, S, -1) for a in post]
    return post, (kv_n[:, :, 0:2], kv_n[:, :, 2:4], qkv_s[:, :, 1:3], win_buf[:, S:])


def kernel(x_prompt, x_sample, mem_prompt, cache_nsa_cmp, cache_nsa_slc, cache_sb, cache_nsa_win,
           cache_mem, page_table, rel_bias, g_attn, w_in, cmp_pe_k, cmp_w1_k, cmp_w2_k, cmp_pe_v,
           cmp_w1_v, cmp_w2_v, g_mem, w_mem_kv, w_br_nsa, w_br_sb, w_br_mem, w_out, g_ffn, w_router,
           b_router, w_up, b_up, w_down, b_down, g_final):
    depth = g_attn.shape[0]
    xp, xs = x_prompt, x_sample
    st_p, st_s = [], []
    for l in range(depth):
        p = {'g_attn': g_attn[l], 'w_in_segs': split_w_in(w_in[l]),
             'cmp_pe_k': cmp_pe_k[l], 'cmp_w1_k': cmp_w1_k[l], 'cmp_w2_k': cmp_w2_k[l],
             'cmp_pe_v': cmp_pe_v[l], 'cmp_w1_v': cmp_w1_v[l], 'cmp_w2_v': cmp_w2_v[l],
             'g_mem': g_mem[l], 'w_mem_kv': w_mem_kv[l],
             'w_br_nsa': w_br_nsa[l], 'w_br_sb': w_br_sb[l], 'w_br_mem': w_br_mem[l], 'w_out': w_out[l],
             'g_ffn': g_ffn[l], 'w_router': w_router[l], 'b_router': b_router[l],
             'w_up': w_up[l], 'b_up': b_up[l], 'w_down': w_down[l], 'b_down': b_down[l]}
        p['cmp_wbd'], p['cmp_w2bd'], p['cmp_pe_term'] = compress_weights(p, jnp.bfloat16)
        xp, sp = prompt_layer(xp, mem_prompt, rel_bias, p)
        xs, ss = sample_layer(xs, cache_nsa_cmp[l], cache_nsa_slc[l], cache_sb[l], cache_nsa_win[l],
                              cache_mem[l], page_table, rel_bias, p)
        xp = ffn_block(*xp, p, g_final, last=(l == depth - 1))
        xs = ffn_block(*xs, p, g_final, last=(l == depth - 1))
        st_p.append(sp)
        st_s.append(ss)
    y_prompt, y_sample = xp, xs

    def stacked(states, i):
        return jnp.stack([s[i] for s in states])

    return (y_prompt, y_sample,
            stacked(st_p, 0), stacked(st_p, 1), stacked(st_p, 2), stacked(st_p, 3), stacked(st_p, 4),
            stacked(st_s, 0), stacked(st_s, 1), stacked(st_s, 2), stacked(st_s, 3))
```

```python
import functools
import math

import jax
import jax.numpy as jnp
import numpy as np
from jax import lax
from jax.experimental import pallas as pl
from jax.experimental.pallas import tpu as pltpu

D_MODEL = 1024
PAGE_SIZE = 128
NSA_HEADS = 8
NSA_KV = 2
NSA_HPG = NSA_HEADS // NSA_KV
HEAD_DIM = 64
CMP_LEN = 32
CMP_STRIDE = 16
CMP_HID = 128
SLC_LEN = 64
SLC_TOP = 16
WINDOW = 512
NSA_QBLK = 32
FORCE_SCORE = 1.0e6
SB_HEADS = 8
SB_QBLK = 128
MEM_HEADS = 4
MEM_HD = 128
REL_BUCKETS = 32
REL_MAX_DIST = 2048
N_EXPERTS = 32
TOP_K = 4
D_FF = 1024
SWIGLU_LIMIT = 7.0
SWIGLU_ALPHA = 1.702
MOE_BLOCK = 256
EPS = 1e-6
NEG_INF = -1.0e30

SEG_QN = NSA_HEADS * HEAD_DIM
SEG_KVN = 6 * NSA_KV * HEAD_DIM
SEG_GN = 3 * NSA_HEADS
SEG_QKVS = 3 * SB_HEADS * HEAD_DIM
SEG_QM = MEM_HEADS * MEM_HD
SEG_GM = 3 * D_MODEL
IN_SIZES = (SEG_QN, SEG_KVN, SEG_GN, SEG_QKVS, SEG_QM, SEG_GM)

VMEM_LIMIT = 48 * 1024 * 1024


def _rms_proj_kernel(x_ref, g_ref, *refs, n_seg, sigmoid):
    w_refs = refs[:n_seg]
    o_refs = refs[n_seg:]
    x = x_ref[...]
    h = x * lax.rsqrt(jnp.mean(x * x, axis=-1, keepdims=True) + EPS) * g_ref[...]
    hb = h.astype(jnp.bfloat16)
    for w_ref, o_ref, sig in zip(w_refs, o_refs, sigmoid):
        z = jnp.dot(hb, w_ref[...], preferred_element_type=jnp.float32)
        if sig:
            z = jax.nn.sigmoid(z)
        o_ref[...] = z.astype(o_ref.dtype)


def rms_proj(x2d, g, weights, sigmoid, out_dtypes, tm):
    n, d = x2d.shape
    assert n % tm == 0
    n_seg = len(weights)
    in_specs = [pl.BlockSpec((tm, d), lambda i: (i, 0)),
                pl.BlockSpec((1, d), lambda i: (0, 0))]
    for w in weights:
        in_specs.append(pl.BlockSpec(w.shape, lambda i: (0, 0), pipeline_mode=pl.Buffered(1)))
    out_specs = [pl.BlockSpec((tm, w.shape[1]), lambda i: (i, 0)) for w in weights]
    out_shape = [jax.ShapeDtypeStruct((n, w.shape[1]), dt) for w, dt in zip(weights, out_dtypes)]
    return pl.pallas_call(
        functools.partial(_rms_proj_kernel, n_seg=n_seg, sigmoid=tuple(sigmoid)),
        grid=(n // tm,),
        in_specs=in_specs,
        out_specs=out_specs,
        out_shape=out_shape,
        compiler_params=pltpu.CompilerParams(dimension_semantics=("parallel",),
                                             vmem_limit_bytes=VMEM_LIMIT),
        name="rms_proj",
    )(x2d, g.reshape(1, d), *weights)


def in_proj(x, g_attn, w_segs):
    B, T, D = x.shape
    n = B * T
    tm = 256 if n % 256 == 0 else n
    outs = rms_proj(x.reshape(n, D), g_attn, w_segs,
                    sigmoid=(False, False, True, False, False, True),
                    out_dtypes=(jnp.float32,) * 6, tm=tm)
    q_n, kv_n, g_n, qkv_s, q_m, g_m = outs
    return (q_n.reshape(B, T, NSA_KV, NSA_HPG, HEAD_DIM),
            kv_n.reshape(B, T, 6, NSA_KV, HEAD_DIM),
            g_n.reshape(B, T, NSA_KV, NSA_HPG, 3),
            qkv_s.reshape(B, T, 3, SB_HEADS, HEAD_DIM),
            q_m.reshape(B, T, MEM_HEADS, MEM_HD),
            g_m.reshape(B, T, 3, D_MODEL))


def split_w_in(w_in):
    offs = np.cumsum((0,) + IN_SIZES)
    return [w_in[:, int(a):int(b)].astype(jnp.bfloat16) for a, b in zip(offs[:-1], offs[1:])]


def _nt_dot(a, b):
    return lax.dot_general(a, b, (((1,), (1,)), ((), ())), preferred_element_type=jnp.float32)


def _online_step(s, mask, v_aug, m, acc, cdt):
    h, tq, tk = s.shape
    s = jnp.where(mask[None], s, NEG_INF)
    m_new = jnp.maximum(m, jnp.max(s, axis=-1, keepdims=True))
    p = jnp.where(mask[None], jnp.exp(s - m_new), 0.0)
    alpha = jnp.exp(m - m_new).reshape(h * tq, 1)
    pv = jnp.dot(p.reshape(h * tq, tk).astype(cdt), v_aug, preferred_element_type=jnp.float32)
    return m_new, alpha * acc + pv


def _finish(acc):
    l = acc[:, HEAD_DIM:HEAD_DIM + 1]
    return acc[:, :HEAD_DIM] / jnp.where(l > 0.0, l, 1.0)


def _nsa_prompt_kernel(q_ref, gate_ref, kc_ref, vc_ref, ks_ref, vs_ref, kw_ref, vw_ref, cb_ref, tb_ref,
                       o_ref, *, tq, n_cmp, ns, cdt):
    f32 = jnp.float32
    qi = pl.program_id(2)
    q0 = qi * tq
    tk = tq
    hpg = NSA_HPG
    rows = hpg * tq
    q = q_ref[0].reshape(rows, HEAD_DIM)

    kc = kc_ref[0, 0]
    ncp = kc.shape[0]
    qpos_c = q0 + lax.broadcasted_iota(jnp.int32, (tq, ncp), 0)
    cidx = lax.broadcasted_iota(jnp.int32, (tq, ncp), 1)
    cmask = (cidx * CMP_STRIDE + (CMP_LEN - 1) <= qpos_c) & (cidx < n_cmp)
    s_c = (_nt_dot(q, kc) + cb_ref[0, 0]).reshape(hpg, tq, ncp)
    s_c = jnp.where(cmask[None], s_c, NEG_INF)
    m_c = jnp.max(s_c, axis=-1, keepdims=True)
    p_c = jnp.where(cmask[None], jnp.exp(s_c - m_c), 0.0)
    l_c = jnp.sum(p_c, axis=-1, keepdims=True)
    p_c = p_c / jnp.where(l_c > 0.0, l_c, 1.0)
    o_c = jnp.dot(p_c.reshape(rows, ncp).astype(cdt), vc_ref[0, 0], preferred_element_type=f32)

    p_sum = p_c[0]
    for h in range(1, hpg):
        p_sum = p_sum + p_c[h]
    jb = lax.broadcasted_iota(jnp.int32, (ns, ncp), 0)
    cc = lax.broadcasted_iota(jnp.int32, (ns, ncp), 1)
    ov_t = ((cc * CMP_STRIDE < jb * SLC_LEN + SLC_LEN)
            & (cc * CMP_STRIDE + (CMP_LEN - 1) >= jb * SLC_LEN)).astype(cdt)
    p_hi = p_sum.astype(cdt)
    p_lo = (p_sum - p_hi.astype(f32)).astype(cdt)
    imp = _nt_dot(ov_t, p_hi) + _nt_dot(ov_t, p_lo)
    blk = lax.broadcasted_iota(jnp.int32, (ns, tq), 0)
    qp_t = q0 + lax.broadcasted_iota(jnp.int32, (ns, tq), 1)
    cur = qp_t // SLC_LEN
    forced = (blk == 0) | (blk == cur) | (blk == cur - 1)
    imp = jnp.where(forced, FORCE_SCORE, imp)
    imp = jnp.where(blk * SLC_LEN <= qp_t, imp, -1.0)
    cnt = jnp.zeros((ns, tq), f32)
    for i in range(ns):
        row = imp[i:i + 1, :]
        ahead = (row > imp) | ((row == imp) & (blk > i))
        cnt = cnt + jnp.where(ahead, 1.0, 0.0)
    sel_t = jnp.where((cnt < float(min(SLC_TOP, ns))) & (imp >= 0.0), 1.0, 0.0)
    sel = sel_t.T.astype(cdt)

    qpos = q0 + lax.broadcasted_iota(jnp.int32, (tq, tk), 0)
    kcol = lax.broadcasted_iota(jnp.int32, (tq, tk), 1)
    eb = lax.broadcasted_iota(jnp.int32, (ns, tk), 0)
    ek = lax.broadcasted_iota(jnp.int32, (ns, tk), 1)
    m0 = jnp.full((hpg, tq, 1), NEG_INF, f32)
    a0 = jnp.zeros((rows, 2 * HEAD_DIM), f32)

    def sel_body(kb, carry):
        m, acc = carry
        k0 = pl.multiple_of(kb * tk, tk)
        s = (_nt_dot(q, ks_ref[0, 0, pl.ds(k0, tk), :]) + tb_ref[0, qi - kb]).reshape(hpg, tq, tk)
        expand = jnp.where(eb == (k0 + ek) // SLC_LEN, 1.0, 0.0).astype(cdt)
        picked = jnp.dot(sel, expand, preferred_element_type=f32)
        mask = (picked > 0.5) & (k0 + kcol <= qpos)
        return _online_step(s, mask, vs_ref[0, 0, pl.ds(k0, tk), :], m, acc, cdt)

    _, acc_s = lax.fori_loop(0, qi + 1, sel_body, (m0, a0))
    o_s = _finish(acc_s)

    def win_body(kb, carry):
        m, acc = carry
        k0 = pl.multiple_of(kb * tk, tk)
        s = (_nt_dot(q, kw_ref[0, 0, pl.ds(k0, tk), :]) + tb_ref[0, qi - kb]).reshape(hpg, tq, tk)
        dist = qpos - (k0 + kcol)
        mask = (dist >= 0) & (dist < WINDOW)
        return _online_step(s, mask, vw_ref[0, 0, pl.ds(k0, tk), :], m, acc, cdt)

    _, acc_w = lax.fori_loop(jnp.maximum(qi - WINDOW // tk, 0), qi + 1, win_body, (m0, a0))
    o_w = _finish(acc_w)

    gates = gate_ref[0, 0]
    outs = []
    for h in range(hpg):
        r = slice(h * tq, (h + 1) * tq)
        outs.append(gates[:, 3 * h:3 * h + 1] * o_c[r]
                    + gates[:, 3 * h + 1:3 * h + 2] * o_s[r]
                    + gates[:, 3 * h + 2:3 * h + 3] * o_w[r])
    o_ref[0] = jnp.concatenate(outs, axis=-1).astype(o_ref.dtype)


def _aug_ones(v):
    return jnp.concatenate([v, jnp.ones_like(v)], axis=-1)


def nsa_bias_tables(rel_bias, t_len, tq):
    nq = t_len // tq
    ncp = t_len // CMP_STRIDE
    G, H = NSA_KV, NSA_HPG
    bias_d = rel_bias[rel_bucket(jnp.arange(t_len, dtype=jnp.int32))]
    bias_d = bias_d.reshape(t_len, G, H).transpose(1, 2, 0)
    q0 = jnp.arange(nq)[:, None] * tq

    def skew(rows, n):
        length = rows.shape[-1]
        lead = rows.shape[:-1]
        flat = jnp.broadcast_to(rows[..., None, :], lead + (n, length)).reshape(lead + (n * length,))
        return flat[..., :n * (length - 1)].reshape(lead + (n, length - 1))

    l = jnp.arange(2 * tq + 1)[None, :]
    w = bias_d[:, :, jnp.clip(q0 + (tq - 1) - l, 0, t_len - 1)]
    tb = skew(w.transpose(0, 2, 1, 3), tq)[..., tq - 1:2 * tq - 1]
    tb = tb.reshape(G, nq, H * tq, tq)
    A = tq // CMP_STRIDE
    b = jnp.arange(CMP_STRIDE)[None, :, None]
    mm = jnp.arange(ncp + A)[None, None, :]
    d_c = q0[:, :, None] + b - ((mm - (A - 1)) * CMP_STRIDE + (CMP_LEN - 1))
    f = bias_d[:, :, jnp.clip(d_c, 0, t_len - 1)]
    cb = skew(f.transpose(0, 2, 1, 3, 4), A)[..., A - 1:A - 1 + ncp]
    cb = cb.transpose(0, 1, 2, 4, 3, 5).reshape(G, nq, H * tq, ncp)
    return tb, cb


def nsa_prompt_pallas(q, gates, kc, vc, k_slc, v_slc, k_win, v_win, rel_bias, *, tq=128, cdt=jnp.bfloat16):
    B, T = q.shape[:2]
    n_cmp = kc.shape[1]
    ncp = T // CMP_STRIDE
    ns = T // SLC_LEN
    nq = T // tq
    rows = NSA_HPG * tq
    tb, cb = nsa_bias_tables(rel_bias, T, tq)
    to_g = lambda a: a.astype(cdt).transpose(0, 2, 1, 3)
    qh = q.astype(cdt).reshape(B, T, NSA_HEADS, HEAD_DIM).transpose(0, 2, 1, 3)
    gt = gates.astype(jnp.float32).reshape(B, T, NSA_KV, NSA_HPG * 3).transpose(0, 2, 1, 3)
    pad_c = ((0, 0), (0, ncp - n_cmp), (0, 0), (0, 0))
    kcg = to_g(jnp.pad(kc, pad_c))
    vcg = to_g(jnp.pad(vc, pad_c))
    ksg, kwg = to_g(k_slc), to_g(k_win)
    vsg, vwg = _aug_ones(to_g(v_slc)), _aug_ones(to_g(v_win))
    per_bg = lambda w: pl.BlockSpec((1, 1, T, w), lambda g, b, i: (b, g, 0, 0))
    per_bg_c = lambda w: pl.BlockSpec((1, 1, ncp, w), lambda g, b, i: (b, g, 0, 0))
    return pl.pallas_call(
        functools.partial(_nsa_prompt_kernel, tq=tq, n_cmp=n_cmp, ns=ns, cdt=cdt),
        grid=(NSA_KV, B, nq),
        in_specs=[
            pl.BlockSpec((1, NSA_HPG, tq, HEAD_DIM), lambda g, b, i: (b, g, i, 0)),
            pl.BlockSpec((1, 1, tq, NSA_HPG * 3), lambda g, b, i: (b, g, i, 0)),
            per_bg_c(HEAD_DIM), per_bg_c(HEAD_DIM),
            per_bg(HEAD_DIM), per_bg(2 * HEAD_DIM), per_bg(HEAD_DIM), per_bg(2 * HEAD_DIM),
            pl.BlockSpec((1, 1, rows, ncp), lambda g, b, i: (g, i, 0, 0)),
            pl.BlockSpec((1, nq, rows, tq), lambda g, b, i: (g, 0, 0, 0), pipeline_mode=pl.Buffered(1)),
        ],
        out_specs=pl.BlockSpec((1, tq, NSA_HPG * HEAD_DIM), lambda g, b, i: (b, i, g)),
        out_shape=jax.ShapeDtypeStruct((B, T, NSA_HEADS * HEAD_DIM), cdt),
        compiler_params=pltpu.CompilerParams(dimension_semantics=("parallel", "parallel", "parallel"),
                                             vmem_limit_bytes=VMEM_LIMIT),
        name="nsa_prompt",
    )(qh, gt, kcg, vcg, ksg, vsg, kwg, vwg, cb, tb)


SB_DEAD_LOG = -104.0


def _split3(x, cdt):
    hi = x.astype(cdt)
    r = x - hi.astype(jnp.float32)
    mid = r.astype(cdt)
    lo = (r - mid.astype(jnp.float32)).astype(cdt)
    return hi, mid, lo


def _sb_tile(z, causal, carry, v, tri, cdt):
    h, tq, tk = z.shape
    sp = jnp.maximum(z, 0.0) + jnp.log1p(jnp.exp(-jnp.abs(z)))
    l1m = jnp.where(causal[None], -sp, 0.0).reshape(h * tq, tk)
    suffix = None
    for part in _split3(l1m, cdt):
        d = jnp.dot(part, tri, preferred_element_type=jnp.float32)
        suffix = d if suffix is None else suffix + d
    surv = (suffix + carry).reshape(h, tq, tk)
    a = jnp.where(causal[None], jnp.exp(z - sp + surv), 0.0)
    pv = jnp.einsum('hqk,hkd->hqd', a.astype(cdt), v, preferred_element_type=jnp.float32)
    return pv, carry + suffix[:, 0:1] + l1m[:, 0:1]


def _sb_prompt_kernel(q_ref, k_ref, v_ref, o_ref, *, tq, hs, cdt):
    f32 = jnp.float32
    qi = pl.program_id(2)
    q0 = qi * tq
    tk = tq
    q = q_ref[0, 0]
    qpos = q0 + lax.broadcasted_iota(jnp.int32, (tq, tk), 0)
    kcol = lax.broadcasted_iota(jnp.int32, (tq, tk), 1)
    tri = jnp.where(lax.broadcasted_iota(jnp.int32, (tk, tk), 0) > lax.broadcasted_iota(jnp.int32, (tk, tk), 1),
                    1.0, 0.0).astype(cdt)

    def cond(state):
        kb, live, _, _ = state
        return (kb >= 0) & (live > SB_DEAD_LOG)

    def body(state):
        kb, _, carry, acc = state
        k0 = pl.multiple_of(kb * tk, tk)
        k = k_ref[0, 0, :, pl.ds(k0, tk), :]
        v = v_ref[0, 0, :, pl.ds(k0, tk), :]
        z = jnp.einsum('hqd,hkd->hqk', q, k, preferred_element_type=f32)
        pv, carry = _sb_tile(z, k0 + kcol < qpos, carry, v, tri, cdt)
        return kb - 1, jnp.max(carry), carry, acc + pv

    init = (qi, jnp.float32(0.0), jnp.zeros((hs * tq, 1), f32), jnp.zeros((hs, tq, HEAD_DIM), f32))
    _, _, _, acc = lax.while_loop(cond, body, init)
    o_ref[0] = jnp.concatenate([acc[h] for h in range(hs)], axis=-1).astype(o_ref.dtype)


def sb_prompt_pallas(qkv, *, tq=128, hs=4, cdt=jnp.bfloat16):
    B, T = qkv.shape[:2]
    x = qkv.astype(cdt).transpose(0, 2, 3, 1, 4)
    full = lambda j: pl.BlockSpec((1, 1, hs, T, HEAD_DIM), lambda b, g, i: (b, j, g, 0, 0))
    return pl.pallas_call(
        functools.partial(_sb_prompt_kernel, tq=tq, hs=hs, cdt=cdt),
        grid=(B, SB_HEADS // hs, T // tq),
        in_specs=[pl.BlockSpec((1, 1, hs, tq, HEAD_DIM), lambda b, g, i: (b, 0, g, i, 0)), full(1), full(2)],
        out_specs=pl.BlockSpec((1, tq, hs * HEAD_DIM), lambda b, g, i: (b, i, g)),
        out_shape=jax.ShapeDtypeStruct((B, T, SB_HEADS * HEAD_DIM), cdt),
        compiler_params=pltpu.CompilerParams(dimension_semantics=("parallel", "parallel", "parallel"),
                                             vmem_limit_bytes=VMEM_LIMIT),
        name="sb_prompt",
    )(x, x, x)


def _moe_kernel(blk_e_ref, n_used_ref, tok_ref, dst_ref, rw_ref, h_hbm, wup_ref, bup_ref, wdn_ref, bdn_ref,
                parts_hbm, xbuf, obuf, sem_in, sem_out, *, blk, cdt):
    j = pl.program_id(0)

    def gather_copy(r):
        return pltpu.make_async_copy(h_hbm.at[pl.ds(tok_ref[0, 0, r], 1)], xbuf.at[pl.ds(r, 1)], sem_in)

    def scatter_copy(r):
        return pltpu.make_async_copy(obuf.at[pl.ds(r, 1)], parts_hbm.at[pl.ds(dst_ref[0, 0, r], 1)], sem_out)

    def each_row(fn):
        def body(r, c):
            fn(r)
            return c
        lax.fori_loop(0, blk, body, 0, unroll=8)

    @pl.when(j < n_used_ref[0])
    def _():
        each_row(lambda r: gather_copy(r).start())
        each_row(lambda r: gather_copy(r).wait())
        x = xbuf[...].astype(cdt)
        gu = jnp.dot(x, wup_ref[0], preferred_element_type=jnp.float32) + bup_ref[0]
        gate = jnp.minimum(gu[:, :D_FF], SWIGLU_LIMIT)
        up = jnp.clip(gu[:, D_FF:], -SWIGLU_LIMIT, SWIGLU_LIMIT)
        act = (up + 1.0) * gate * jax.nn.sigmoid(SWIGLU_ALPHA * gate)
        out = jnp.dot(act.astype(cdt), wdn_ref[0], preferred_element_type=jnp.float32) + bdn_ref[0]
        obuf[...] = out * rw_ref[...]
        each_row(lambda r: scatter_copy(r).start())
        each_row(lambda r: scatter_copy(r).wait())


def moe_ffn_pallas(h_pad, blk_e, n_used, row_tok, row_dst, row_w, w_up, b_up, w_down, b_down, *, blk, n_assign,
                   cdt=jnp.bfloat16):
    n_rows = row_tok.shape[0]
    n_blocks = n_rows // blk
    d = h_pad.shape[1]
    smem_rows = lambda: pl.BlockSpec((1, 1, blk), lambda j, be, nu: (j, 0, 0), memory_space=pltpu.SMEM)
    grid_spec = pltpu.PrefetchScalarGridSpec(
        num_scalar_prefetch=2,
        grid=(n_blocks,),
        in_specs=[
            smem_rows(), smem_rows(),
            pl.BlockSpec((blk, 1), lambda j, be, nu: (j, 0)),
            pl.BlockSpec(memory_space=pl.ANY),
            pl.BlockSpec((1, d, 2 * D_FF), lambda j, be, nu: (be[j], 0, 0)),
            pl.BlockSpec((1, 1, 2 * D_FF), lambda j, be, nu: (be[j], 0, 0)),
            pl.BlockSpec((1, D_FF, d), lambda j, be, nu: (be[j], 0, 0)),
            pl.BlockSpec((1, 1, d), lambda j, be, nu: (be[j], 0, 0)),
        ],
        out_specs=pl.BlockSpec(memory_space=pl.ANY),
        scratch_shapes=[pltpu.VMEM((blk, d), jnp.float32), pltpu.VMEM((blk, d), jnp.float32),
                        pltpu.SemaphoreType.DMA(()), pltpu.SemaphoreType.DMA(())],
    )
    return pl.pallas_call(
        functools.partial(_moe_kernel, blk=blk, cdt=cdt),
        grid_spec=grid_spec,
        out_shape=jax.ShapeDtypeStruct((n_assign + blk, d), jnp.float32),
        compiler_params=pltpu.CompilerParams(dimension_semantics=("arbitrary",), vmem_limit_bytes=VMEM_LIMIT),
        name="moe_ffn",
    )(blk_e, n_used, row_tok.reshape(n_blocks, 1, blk), row_dst.reshape(n_blocks, 1, blk),
      row_w.reshape(n_rows, 1), h_pad, w_up.astype(cdt), b_up.reshape(N_EXPERTS, 1, 2 * D_FF),
      w_down.astype(cdt), b_down.reshape(N_EXPERTS, 1, d))


def _post_attn_kernel(x_ref, on_ref, os_ref, m_ref, gm_ref, memkv_ref, wn_ref, ws_ref, wm_ref, wo_ref, gf_ref,
                      wr_ref, br_ref, x1_ref, hf_ref, lg_ref, *, with_mem, cdt):
    f32 = jnp.float32
    if with_mem:
        qm = m_ref[0].astype(cdt)
        mem = memkv_ref[0]
        half = MEM_HEADS * MEM_HD
        heads = []
        for h in range(MEM_HEADS):
            c = slice(h * MEM_HD, (h + 1) * MEM_HD)
            s = _nt_dot(qm[:, c], mem[:, c]) * (MEM_HD ** -0.5)
            pm = jnp.exp(s - jnp.max(s, axis=-1, keepdims=True))
            pm = pm / jnp.sum(pm, axis=-1, keepdims=True)
            heads.append(jnp.dot(pm.astype(cdt), mem[:, half + h * MEM_HD:half + (h + 1) * MEM_HD],
                                 preferred_element_type=f32))
        o_m = jnp.concatenate(heads, axis=-1)
    else:
        o_m = m_ref[0]
    d = x_ref.shape[-1]
    gm = gm_ref[0]
    y = (gm[:, :d] * jnp.dot(on_ref[0].astype(cdt), wn_ref[...], preferred_element_type=f32)
         + gm[:, d:2 * d] * jnp.dot(os_ref[0].astype(cdt), ws_ref[...], preferred_element_type=f32)
         + gm[:, 2 * d:] * jnp.dot(o_m.astype(cdt), wm_ref[...], preferred_element_type=f32))
    x1 = x_ref[0] + jnp.dot(y.astype(cdt), wo_ref[...], preferred_element_type=f32)
    x1_ref[0] = x1
    hf = x1 * lax.rsqrt(jnp.mean(x1 * x1, axis=-1, keepdims=True) + EPS) * gf_ref[...]
    hf_ref[0] = hf
    lg_ref[0] = jnp.dot(hf.astype(cdt), wr_ref[...], preferred_element_type=f32) + br_ref[...]


def post_attn_pallas(x, o_n, o_s, m, g_m, mem_kv, p, *, with_mem, cdt=jnp.bfloat16):
    B, T, d = x.shape
    tm = 256 if T % 256 == 0 else T
    if mem_kv is None:
        mem_kv = jnp.zeros((B, 8, 2 * MEM_HEADS * MEM_HD), cdt)
    ml = mem_kv.shape[1]
    rows = lambda w: pl.BlockSpec((1, tm, w), lambda b, i: (b, i, 0))
    const = lambda a: pl.BlockSpec(a.shape, lambda b, i: (0,) * a.ndim)
    weights = [p['w_br_nsa'].astype(cdt), p['w_br_sb'].astype(cdt), p['w_br_mem'].astype(cdt),
               p['w_out'].astype(cdt), p['g_ffn'].reshape(1, d), p['w_router'].astype(cdt),
               p['b_router'].reshape(1, N_EXPERTS)]
    return pl.pallas_call(
        functools.partial(_post_attn_kernel, with_mem=with_mem, cdt=cdt),
        grid=(B, T // tm),
        in_specs=[rows(d), rows(o_n.shape[-1]), rows(o_s.shape[-1]), rows(m.shape[-1]), rows(3 * d),
                  pl.BlockSpec((1, ml, mem_kv.shape[-1]), lambda b, i: (b, 0, 0))] + [const(w) for w in weights],
        out_specs=[rows(d), rows(d), rows(N_EXPERTS)],
        out_shape=[jax.ShapeDtypeStruct((B, T, d), jnp.float32), jax.ShapeDtypeStruct((B, T, d), jnp.float32),
                   jax.ShapeDtypeStruct((B, T, N_EXPERTS), jnp.float32)],
        compiler_params=pltpu.CompilerParams(dimension_semantics=("parallel", "parallel"),
                                             vmem_limit_bytes=VMEM_LIMIT),
        name="post_attn",
    )(x, o_n, o_s, m, g_m, mem_kv.astype(cdt), *weights)


def _combine_norm_kernel(x_ref, *refs, normalize):
    part_refs, g_ref, o_ref = refs[:TOP_K], refs[TOP_K], refs[TOP_K + 1]
    x = x_ref[...]
    for part_ref in part_refs:
        x = x + part_ref[...]
    if normalize:
        x = x * lax.rsqrt(jnp.mean(x * x, axis=-1, keepdims=True) + EPS) * g_ref[...]
    o_ref[...] = x


def combine_norm_pallas(x, parts, g_final, tm, normalize):
    n, d = x.shape
    nt = n // tm
    part_spec = lambda k: pl.BlockSpec((tm, d), lambda i: (k * nt + i, 0))
    return pl.pallas_call(
        functools.partial(_combine_norm_kernel, normalize=normalize),
        grid=(nt,),
        in_specs=[pl.BlockSpec((tm, d), lambda i: (i, 0))] + [part_spec(k) for k in range(TOP_K)]
                 + [pl.BlockSpec((1, d), lambda i: (0, 0))],
        out_specs=pl.BlockSpec((tm, d), lambda i: (i, 0)),
        out_shape=jax.ShapeDtypeStruct((n, d), jnp.float32),
        compiler_params=pltpu.CompilerParams(dimension_semantics=("parallel",), vmem_limit_bytes=VMEM_LIMIT),
        name="combine_norm",
    )(x, *([parts] * TOP_K), g_final.reshape(1, d))


def ffn_block(x, hf, logits, p, g_final, last):
    lead = x.shape[:-1]
    x2 = x.reshape(-1, D_MODEL)
    hf = hf.reshape(-1, D_MODEL)
    logits = logits.reshape(-1, N_EXPERTS)
    N = x2.shape[0]
    top_l, top_e = lax.top_k(logits, TOP_K)
    top_w = jax.nn.softmax(top_l, axis=-1)
    n_assign = N * TOP_K
    blk = int(min(MOE_BLOCK, max(8, n_assign // N_EXPERTS)))
    n_rows = (n_assign // blk + N_EXPERTS) * blk
    n_blocks = n_rows // blk
    flat_e = top_e.reshape(-1)
    order = jnp.argsort(flat_e).astype(jnp.int32)
    se = flat_e[order]
    counts = jnp.bincount(flat_e, length=N_EXPERTS)
    start = jnp.cumsum(counts) - counts
    padded = (counts + blk - 1) // blk * blk
    p_end = jnp.cumsum(padded)
    p_start = p_end - padded
    dest = p_start[se] + jnp.arange(n_assign) - start[se]
    aid = jnp.full((n_rows,), n_assign, jnp.int32).at[dest].set(order)
    real = aid < n_assign
    row_tok = jnp.where(real, aid // TOP_K, N)
    row_dst = jnp.where(real, (aid % TOP_K) * N + aid // TOP_K, n_assign + jnp.arange(n_rows) % blk)
    row_w = jnp.where(real, top_w.reshape(-1)[jnp.minimum(aid, n_assign - 1)], 0.0)
    blk_e = jnp.minimum(jnp.searchsorted(p_end, jnp.arange(n_blocks) * blk, side='right'),
                        N_EXPERTS - 1).astype(jnp.int32)
    n_used = (p_end[-1] // blk).astype(jnp.int32).reshape(1)
    h_pad = jnp.concatenate([hf, jnp.zeros((1, D_MODEL), hf.dtype)], axis=0)
    parts = moe_ffn_pallas(h_pad, blk_e, n_used, row_tok, row_dst, row_w, p['w_up'], p['b_up'],
                           p['w_down'], p['b_down'], blk=blk, n_assign=n_assign)
    tm = 256 if N % 256 == 0 else N
    y = combine_norm_pallas(x2, parts, g_final, tm, normalize=last)
    return y.reshape(*lead, D_MODEL)


def _rows_matmul_kernel(x_ref, w_ref, o_ref):
    o_ref[...] = jnp.dot(x_ref[...].astype(w_ref.dtype), w_ref[...], preferred_element_type=jnp.float32)


def rows_matmul(x, w, tm):
    r, k = x.shape
    n = w.shape[1]
    return pl.pallas_call(
        _rows_matmul_kernel,
        grid=(r // tm,),
        in_specs=[pl.BlockSpec((tm, k), lambda i: (i, 0)),
                  pl.BlockSpec((k, n), lambda i: (0, 0), pipeline_mode=pl.Buffered(1))],
        out_specs=pl.BlockSpec((tm, n), lambda i: (i, 0)),
        out_shape=jax.ShapeDtypeStruct((r, n), jnp.float32),
        compiler_params=pltpu.CompilerParams(dimension_semantics=("parallel",), vmem_limit_bytes=VMEM_LIMIT),
        name="rows_matmul",
    )(x, w)


def _halves_matmul_kernel(*refs):
    x_refs, w_ref, o_ref = refs[:-2], refs[-2], refs[-1]
    n = o_ref.shape[0]
    pages, page = x_refs[0].shape[:2]
    strips = len(x_refs)
    acc = None
    for s in range(CMP_STRIDE):
        for c, x_ref in enumerate(x_refs):
            xs = x_ref[:, pl.ds(s, page // CMP_STRIDE, stride=CMP_STRIDE), :].reshape(n, x_ref.shape[2])
            d = jnp.dot(xs.astype(w_ref.dtype), w_ref[s * strips + c], preferred_element_type=jnp.float32)
            acc = d if acc is None else acc + d
    o_ref[...] = acc


def halves_matmul(x, wbd, pages_per_step):
    n_pg, page, width = x.shape
    n = wbd.shape[1]
    lane = 128
    strips = width // lane
    strip_spec = lambda c: pl.BlockSpec((pages_per_step, page, lane), lambda i: (i, 0, c))
    return pl.pallas_call(
        _halves_matmul_kernel,
        grid=(n_pg // pages_per_step,),
        in_specs=[strip_spec(c) for c in range(strips)]
                 + [pl.BlockSpec((CMP_STRIDE * strips, lane, n), lambda i: (0, 0, 0), pipeline_mode=pl.Buffered(1))],
        out_specs=pl.BlockSpec((pages_per_step * page // CMP_STRIDE, n), lambda i: (i, 0)),
        out_shape=jax.ShapeDtypeStruct((n_pg * page // CMP_STRIDE, n), jnp.float32),
        compiler_params=pltpu.CompilerParams(dimension_semantics=("parallel",), vmem_limit_bytes=VMEM_LIMIT),
        name="halves_matmul",
    )(*([x] * strips), wbd.reshape(CMP_STRIDE * strips, lane, n))


def compress_weights(p, cdt):
    R = CMP_LEN // CMP_STRIDE
    nc4 = 2 * NSA_KV
    w1 = jnp.stack([p['cmp_w1_k'], p['cmp_w1_k'], p['cmp_w1_v'], p['cmp_w1_v']])
    w1 = w1.reshape(nc4, R, CMP_STRIDE, HEAD_DIM, CMP_HID)
    eye = jnp.eye(nc4, dtype=w1.dtype)
    wbd = jnp.einsum('crsdh,ce->scdreh', w1, eye).reshape(CMP_STRIDE * nc4 * HEAD_DIM, R * nc4 * CMP_HID)
    w2 = jnp.stack([p['cmp_w2_k'], p['cmp_w2_k'], p['cmp_w2_v'], p['cmp_w2_v']])
    w2bd = jnp.einsum('chd,ce->ched', w2, eye).reshape(nc4 * CMP_HID, nc4 * HEAD_DIM)
    pe = jnp.stack([p['cmp_pe_k'].reshape(-1) @ p['cmp_w1_k'], p['cmp_pe_k'].reshape(-1) @ p['cmp_w1_k'],
                    p['cmp_pe_v'].reshape(-1) @ p['cmp_w1_v'], p['cmp_pe_v'].reshape(-1) @ p['cmp_w1_v']])
    return wbd.astype(cdt), w2bd.astype(cdt), pe.reshape(1, nc4 * CMP_HID)


def rows_to_columns(pages):
    return pages.transpose(0, 2, 1)


def _softmax_with_extra(s, mask, s_x):
    s = jnp.where(mask, s, NEG_INF)
    m = jnp.maximum(jnp.max(s, axis=-1, keepdims=True), s_x)
    p = jnp.where(mask, jnp.exp(s - m), 0.0)
    p_x = jnp.exp(s_x - m)
    inv = 1.0 / (jnp.sum(p, axis=-1, keepdims=True) + p_x)
    return p * inv, p_x * inv


def _nsa_decode_kernel(pt_ref, qk_ref, gate_ref, news_ref, neww_ref, win_ref, pe_ref, w2_ref, cbias_ref,
                       sbias_ref, wbias_ref, b0_ref, e_ref, h_hbm, slc_hbm, o_ref, hbuf, sbuf, sem,
                       *, n_pages, n_cmp, ns, cdt):
    f32 = jnp.float32
    b = pl.program_id(0)
    nb = pl.num_programs(0)
    slot = b % 2
    hpp = PAGE_SIZE // CMP_STRIDE

    def copies(bb, sl):
        out = []
        for p in range(n_pages):
            pg = pt_ref[bb, p]
            out.append(pltpu.make_async_copy(h_hbm.at[pg], hbuf.at[sl, pl.ds(p * hpp, hpp)], sem.at[0, sl]))
            out.append(pltpu.make_async_copy(slc_hbm.at[pg], sbuf.at[sl, :, pl.ds(p * PAGE_SIZE, PAGE_SIZE)],
                                             sem.at[1, sl]))
        return out

    @pl.when(b == 0)
    def _():
        for c in copies(0, 0):
            c.start()

    @pl.when(b + 1 < nb)
    def _():
        for c in copies(b + 1, 1 - slot):
            c.start()

    for c in copies(b, slot):
        c.wait()

    qk = qk_ref[0]
    nh = qk.shape[0]
    width = qk.shape[1]
    row = lax.broadcasted_iota(jnp.int32, (nh, 1), 0)

    def own_values(o_full):
        v0 = o_full[:, NSA_KV * HEAD_DIM:NSA_KV * HEAD_DIM + HEAD_DIM]
        v1 = o_full[:, NSA_KV * HEAD_DIM + HEAD_DIM:]
        return jnp.where(row < NSA_HPG, v0, v1)

    hh = hbuf[slot]
    ncp = hh.shape[0]
    half = hh.shape[1] // 2
    pre = hh[:, :half] + pltpu.roll(hh[:, half:], ncp - 1, 0) + pe_ref[...]
    kvc = jnp.dot(jax.nn.gelu(pre).astype(cdt), w2_ref[...], preferred_element_type=f32).astype(cdt)
    cidx = lax.broadcasted_iota(jnp.int32, (nh, ncp), 1)
    cmask = cidx < n_cmp
    s_c = jnp.where(cmask, _nt_dot(qk, kvc) + cbias_ref[...], NEG_INF)
    m_c = jnp.max(s_c, axis=-1, keepdims=True)
    p_c = jnp.where(cmask, jnp.exp(s_c - m_c), 0.0)
    p_c = p_c / jnp.sum(p_c, axis=-1, keepdims=True)
    o_c = own_values(jnp.dot(p_c.astype(cdt), kvc, preferred_element_type=f32))

    sum0 = jnp.sum(jnp.where(row < NSA_HPG, p_c, 0.0), axis=0, keepdims=True)
    sum1 = jnp.sum(jnp.where(row < NSA_HPG, 0.0, p_c), axis=0, keepdims=True)
    p_sum = jnp.where(row < NSA_HPG, sum0, sum1)
    nsp = e_ref.shape[0]
    cc = lax.broadcasted_iota(jnp.int32, (ncp, nsp), 0)
    jb = lax.broadcasted_iota(jnp.int32, (ncp, nsp), 1)
    ov = ((cc * CMP_STRIDE < jb * SLC_LEN + SLC_LEN)
          & (cc * CMP_STRIDE + (CMP_LEN - 1) >= jb * SLC_LEN)).astype(cdt)
    p_hi = p_sum.astype(cdt)
    p_lo = (p_sum - p_hi.astype(f32)).astype(cdt)
    imp = jnp.dot(p_hi, ov, preferred_element_type=f32) + jnp.dot(p_lo, ov, preferred_element_type=f32)
    past = n_pages * PAGE_SIZE
    blk = lax.broadcasted_iota(jnp.int32, (nh, nsp), 1)
    cur = past // SLC_LEN
    forced = (blk == 0) | (blk == cur) | (blk == cur - 1)
    imp = jnp.where(forced, FORCE_SCORE, imp)
    imp = jnp.where(blk < ns, imp, -1.0)
    imp_t = imp.T
    bi = lax.broadcasted_iota(jnp.int32, (nsp, nsp), 0)
    bj = lax.broadcasted_iota(jnp.int32, (nsp, nsp), 1)
    sel_rows = []
    for g in range(NSA_KV):
        col = imp_t[:, g * NSA_HPG:g * NSA_HPG + 1]
        rw = imp[g * NSA_HPG:g * NSA_HPG + 1, :]
        ahead = (col > rw) | ((col == rw) & (bi < bj))
        cnt = jnp.sum(jnp.where(ahead, 1.0, 0.0), axis=0, keepdims=True)
        sel_rows.append(jnp.where((cnt < float(min(SLC_TOP, ns))) & (rw >= 0.0), 1.0, 0.0))
    sel = jnp.where(row < NSA_HPG, sel_rows[0], sel_rows[1])

    keys_t = sbuf[slot].astype(cdt)
    picked = jnp.dot(sel.astype(cdt), e_ref[...], preferred_element_type=f32)
    new_s = news_ref[0]
    s_x = jnp.sum(qk.astype(f32) * new_s.astype(cdt).astype(f32), axis=-1, keepdims=True) + b0_ref[...]
    p_s, p_x = _softmax_with_extra(jnp.dot(qk, keys_t, preferred_element_type=f32) + sbias_ref[...],
                                   picked > 0.5, s_x)
    o_s = own_values(_nt_dot(p_s.astype(cdt), keys_t)
                     + p_x.astype(cdt).astype(f32) * new_s.astype(cdt).astype(f32))

    wrows_t = win_ref[0].astype(cdt)
    wb = wrows_t.shape[1]
    new_w = neww_ref[0]
    widx = lax.broadcasted_iota(jnp.int32, (nh, wb), 1)
    s_xw = jnp.sum(qk.astype(f32) * new_w.astype(cdt).astype(f32), axis=-1, keepdims=True) + b0_ref[...]
    p_w, p_xw = _softmax_with_extra(jnp.dot(qk, wrows_t, preferred_element_type=f32) + wbias_ref[...],
                                    wb - widx < WINDOW, s_xw)
    o_w = own_values(_nt_dot(p_w.astype(cdt), wrows_t)
                     + p_xw.astype(cdt).astype(f32) * new_w.astype(cdt).astype(f32))

    gates = gate_ref[0]
    o_ref[0] = gates[:, 0:1] * o_c + gates[:, 1:2] * o_s + gates[:, 2:3] * o_w


def nsa_decode_pallas(q_n, gates, kv_new, c_win, h_all, c_slc, page_table, rel_bias, w2bd, pe_term,
                      cdt=jnp.bfloat16):
    B = q_n.shape[0]
    n_pages = page_table.shape[1]
    past = n_pages * PAGE_SIZE
    wb = c_win.shape[1]
    hpp = PAGE_SIZE // CMP_STRIDE
    ncp = n_pages * hpp
    n_cmp = ncp - (CMP_LEN // CMP_STRIDE - 1)
    ns = -(-(past + 1) // SLC_LEN)
    nsp = -(-ns // 128) * 128
    width = 2 * NSA_KV * HEAD_DIM
    H = NSA_HEADS
    qs = (q_n * HEAD_DIM ** -0.5).reshape(B, NSA_KV, NSA_HPG, HEAD_DIM)
    qk = jnp.einsum('bghd,ge->bghed', qs, jnp.eye(NSA_KV, dtype=qs.dtype)).reshape(B, H, NSA_KV * HEAD_DIM)
    qk = jnp.pad(qk, ((0, 0), (0, 0), (0, width - NSA_KV * HEAD_DIM))).astype(cdt)
    bias_d = rel_bias[rel_bucket(jnp.arange(past + 1, dtype=jnp.int32))].T
    c_end = jnp.arange(ncp) * CMP_STRIDE + (CMP_LEN - 1)
    cbias = bias_d[:, jnp.clip(past - c_end, 0, past)]
    sbias = bias_d[:, past - jnp.arange(past)]
    wbias = bias_d[:, jnp.clip(wb - jnp.arange(wb), 0, past)]
    b0 = bias_d[:, 0:1]
    expand = (jnp.arange(nsp)[:, None] == (jnp.arange(past) // SLC_LEN)[None, :]).astype(cdt)
    const = lambda shape: pl.BlockSpec(shape, lambda b, pt: (0,) * len(shape))
    per_b = lambda shape: pl.BlockSpec((1,) + shape, lambda b, pt: (b,) + (0,) * len(shape))
    grid_spec = pltpu.PrefetchScalarGridSpec(
        num_scalar_prefetch=1,
        grid=(B,),
        in_specs=[per_b((H, width)), per_b((H, 3)), per_b((1, width)), per_b((1, width)), per_b((width, wb)),
                  const((1, pe_term.shape[1])), const(w2bd.shape), const((H, ncp)), const((H, past)),
                  const((H, wb)), const((H, 1)),
                  pl.BlockSpec((nsp, past), lambda b, pt: (0, 0), pipeline_mode=pl.Buffered(1)),
                  pl.BlockSpec(memory_space=pl.ANY), pl.BlockSpec(memory_space=pl.ANY)],
        out_specs=per_b((H, HEAD_DIM)),
        scratch_shapes=[pltpu.VMEM((2, ncp, h_all.shape[2]), jnp.float32),
                        pltpu.VMEM((2, width, past), jnp.float32),
                        pltpu.SemaphoreType.DMA((2, 2))],
    )
    out = pl.pallas_call(
        functools.partial(_nsa_decode_kernel, n_pages=n_pages, n_cmp=n_cmp, ns=ns, cdt=cdt),
        grid_spec=grid_spec,
        out_shape=jax.ShapeDtypeStruct((B, H, HEAD_DIM), jnp.float32),
        compiler_params=pltpu.CompilerParams(dimension_semantics=("arbitrary",), vmem_limit_bytes=VMEM_LIMIT),
        name="nsa_decode",
    )(page_table, qk, gates.reshape(B, H, 3), kv_new[:, 2:4].reshape(B, 1, width),
      kv_new[:, 4:6].reshape(B, 1, width), rows_to_columns(c_win.reshape(B, wb, width)), pe_term, w2bd, cbias,
      sbias, wbias, b0, expand, h_all, rows_to_columns(c_slc.reshape(c_slc.shape[0], PAGE_SIZE, width)))
    return out.reshape(B, H * HEAD_DIM)


def _compress_finish_kernel(h_ref, pe_ref, w2_ref, o_ref):
    hh = h_ref[0]
    nh = hh.shape[0]
    half = hh.shape[1] // 2
    pre = hh[:, :half] + pltpu.roll(hh[:, half:], nh - 1, 0) + pe_ref[...]
    o_ref[0] = jnp.dot(jax.nn.gelu(pre).astype(w2_ref.dtype), w2_ref[...], preferred_element_type=jnp.float32)


def compress_prompt_pallas(cmp_rows, wbd, w2bd, pe_term):
    B, T = cmp_rows.shape[:2]
    nh = T // CMP_STRIDE
    width = 2 * NSA_KV * HEAD_DIM
    rows = B * nh
    tm = next(t for t in (512, 256, 128, 64, 32, 16, 8) if rows % t == 0)
    h = rows_matmul(cmp_rows.reshape(rows, CMP_STRIDE * width), wbd, tm).reshape(B, nh, wbd.shape[1])
    out = pl.pallas_call(
        _compress_finish_kernel,
        grid=(B,),
        in_specs=[pl.BlockSpec((1, nh, wbd.shape[1]), lambda b: (b, 0, 0)),
                  pl.BlockSpec(pe_term.shape, lambda b: (0, 0)),
                  pl.BlockSpec(w2bd.shape, lambda b: (0, 0))],
        out_specs=pl.BlockSpec((1, nh, width), lambda b: (b, 0, 0)),
        out_shape=jax.ShapeDtypeStruct((B, nh, width), jnp.float32),
        compiler_params=pltpu.CompilerParams(dimension_semantics=("parallel",), vmem_limit_bytes=VMEM_LIMIT),
        name="compress_finish",
    )(h, pe_term, w2bd)
    return out.reshape(B, nh, 2, NSA_KV, HEAD_DIM)


SB_CHUNK_PAGES = 2


def _sb_mem_decode_kernel(pt_ref, qs_ref, qm_ref, mem_ref, sb_hbm, os_ref, om_ref, first, extra, sem,
                          *, n_pages, cdt):
    f32 = jnp.float32
    b = pl.program_id(0)
    nb = pl.num_programs(0)
    slot = b % 2
    ch = SB_CHUNK_PAGES
    ck = ch * PAGE_SIZE
    n_chunks = n_pages // ch

    def copies(bb, c, dst, s):
        return [pltpu.make_async_copy(sb_hbm.at[pt_ref[bb, n_pages - (c + 1) * ch + i]],
                                      dst.at[:, pl.ds(i * PAGE_SIZE, PAGE_SIZE)], s) for i in range(ch)]

    @pl.when(b == 0)
    def _():
        for c in copies(0, 0, first.at[0], sem.at[0]):
            c.start()

    @pl.when(b + 1 < nb)
    def _():
        for c in copies(b + 1, 0, first.at[1 - slot], sem.at[1 - slot]):
            c.start()

    for c in copies(b, 0, first.at[slot], sem.at[slot]):
        c.wait()

    qs = qs_ref[0]
    nh = qs.shape[0]
    tri = jnp.where(lax.broadcasted_iota(jnp.int32, (ck, ck), 0) > lax.broadcasted_iota(jnp.int32, (ck, ck), 1),
                    1.0, 0.0).astype(cdt)

    def chunk(rows, carry, acc):
        keys_t = rows.astype(cdt)
        z = jnp.dot(qs, keys_t, preferred_element_type=f32)
        sp = jnp.maximum(z, 0.0) + jnp.log1p(jnp.exp(-jnp.abs(z)))
        l1m = -sp
        suffix = None
        for part in _split3(l1m, cdt):
            d = jnp.dot(part, tri, preferred_element_type=f32)
            suffix = d if suffix is None else suffix + d
        a = jnp.exp(z - sp + suffix + carry)
        acc = acc + _nt_dot(a.astype(cdt), keys_t)
        return carry + suffix[:, 0:1] + l1m[:, 0:1], acc

    carry, acc = chunk(first[slot], jnp.zeros((nh, 1), f32), jnp.zeros((nh, qs.shape[1]), f32))

    def cond(state):
        c, live, _, _ = state
        return (c < n_chunks) & (live > SB_DEAD_LOG)

    def body(state):
        c, _, carry, acc = state
        for cp in copies(b, c, extra, sem.at[2]):
            cp.start()
        for cp in copies(b, c, extra, sem.at[2]):
            cp.wait()
        carry, acc = chunk(extra[...], carry, acc)
        return c + 1, jnp.max(carry), carry, acc

    _, _, _, acc = lax.while_loop(cond, body, (jnp.int32(1), jnp.max(carry), carry, acc))
    v0 = SB_HEADS * HEAD_DIM
    os_ref[0] = jnp.concatenate([acc[h:h + 1, v0 + h * HEAD_DIM:v0 + (h + 1) * HEAD_DIM] for h in range(nh)], axis=0)

    mem = mem_ref[0].astype(cdt)
    s = _nt_dot(qm_ref[0], mem) * (MEM_HD ** -0.5)
    p = jnp.exp(s - jnp.max(s, axis=-1, keepdims=True))
    p = p / jnp.sum(p, axis=-1, keepdims=True)
    om = jnp.dot(p.astype(cdt), mem, preferred_element_type=f32)
    m0 = MEM_HEADS * MEM_HD
    om_ref[0] = jnp.concatenate([om[h:h + 1, m0 + h * MEM_HD:m0 + (h + 1) * MEM_HD] for h in range(MEM_HEADS)], axis=0)


def sb_mem_decode_pallas(q_s, q_m, c_sb, c_mem, page_table, cdt=jnp.bfloat16):
    B = q_s.shape[0]
    n_pages = page_table.shape[1]
    ws = 2 * SB_HEADS * HEAD_DIM
    wm = 2 * MEM_HEADS * MEM_HD
    ml = c_mem.shape[1]
    qs = jnp.einsum('bhd,he->bhed', q_s * HEAD_DIM ** -0.5, jnp.eye(SB_HEADS, dtype=q_s.dtype))
    qs = jnp.pad(qs.reshape(B, SB_HEADS, ws // 2), ((0, 0), (0, 0), (0, ws // 2))).astype(cdt)
    qm = jnp.einsum('bhd,he->bhed', q_m, jnp.eye(MEM_HEADS, dtype=q_m.dtype)).reshape(B, MEM_HEADS, wm // 2)
    qm = jnp.pad(qm, ((0, 0), (0, 8 - MEM_HEADS), (0, wm // 2))).astype(cdt)
    per_b = lambda shape: pl.BlockSpec((1,) + shape, lambda b, pt: (b,) + (0,) * len(shape))
    ck = SB_CHUNK_PAGES * PAGE_SIZE
    grid_spec = pltpu.PrefetchScalarGridSpec(
        num_scalar_prefetch=1,
        grid=(B,),
        in_specs=[per_b((SB_HEADS, ws)), per_b((8, wm)), per_b((ml, wm)), pl.BlockSpec(memory_space=pl.ANY)],
        out_specs=[per_b((SB_HEADS, HEAD_DIM)), per_b((MEM_HEADS, MEM_HD))],
        scratch_shapes=[pltpu.VMEM((2, ws, ck), jnp.float32), pltpu.VMEM((ws, ck), jnp.float32),
                        pltpu.SemaphoreType.DMA((3,))],
    )
    o_s, o_m = pl.pallas_call(
        functools.partial(_sb_mem_decode_kernel, n_pages=n_pages, cdt=cdt),
        grid_spec=grid_spec,
        out_shape=[jax.ShapeDtypeStruct((B, SB_HEADS, HEAD_DIM), jnp.float32),
                   jax.ShapeDtypeStruct((B, MEM_HEADS, MEM_HD), jnp.float32)],
        compiler_params=pltpu.CompilerParams(dimension_semantics=("arbitrary",), vmem_limit_bytes=VMEM_LIMIT),
        name="sb_mem_decode",
    )(page_table, qs, qm, c_mem.reshape(B, ml, wm), rows_to_columns(c_sb.reshape(c_sb.shape[0], PAGE_SIZE, ws)))
    return o_s.reshape(B, SB_HEADS * HEAD_DIM), o_m.reshape(B, MEM_HEADS * MEM_HD)


def rel_bucket(dist):
    n = jnp.maximum(dist, 0)
    max_exact = REL_BUCKETS // 2
    nf = jnp.maximum(n, 1).astype(jnp.float32)
    large = max_exact + (jnp.log(nf / max_exact) / math.log(REL_MAX_DIST / max_exact)
                         * (REL_BUCKETS - max_exact)).astype(jnp.int32)
    return jnp.where(n < max_exact, n, jnp.minimum(large, REL_BUCKETS - 1))


def prompt_layer(x, mem, rel_bias, p):
    B, T = x.shape[:2]
    q_n, kv_n, g_n, qkv_s, q_m, g_m = in_proj(x, p['g_attn'], p['w_in_segs'])
    n_cmp = T // CMP_STRIDE - (CMP_LEN // CMP_STRIDE - 1)
    kvc = compress_prompt_pallas(kv_n[:, :, 0:2], p['cmp_wbd'], p['cmp_w2bd'], p['cmp_pe_term'])[:, :n_cmp]
    att_scale = HEAD_DIM ** -0.5
    o_n = nsa_prompt_pallas(q_n * att_scale, g_n, kvc[:, :, 0], kvc[:, :, 1], kv_n[:, :, 2], kv_n[:, :, 3],
                            kv_n[:, :, 4], kv_n[:, :, 5], rel_bias)
    o_s = sb_prompt_pallas(qkv_s * jnp.array([att_scale, 1.0, 1.0], jnp.float32).reshape(1, 1, 3, 1, 1))
    ml = mem.shape[1]
    mem_rows = B * ml
    tm_mem = 256 if mem_rows % 256 == 0 else mem_rows
    mem_kv = rms_proj(mem.reshape(mem_rows, D_MODEL), p['g_mem'], [p['w_mem_kv'].astype(jnp.bfloat16)],
                      sigmoid=(False,), out_dtypes=(jnp.float32,), tm=tm_mem)[0].reshape(B, ml, -1)
    post = post_attn_pallas(x, o_n, o_s, q_m.reshape(B, T, -1), g_m.reshape(B, T, -1), mem_kv, p, with_mem=True)
    wb = min(WINDOW, T)
    return post, (kv_n[:, :, 0:2], kv_n[:, :, 2:4], qkv_s[:, :, 1:3], kv_n[:, T - wb:, 4:6],
                  mem_kv.reshape(B, ml, 2, MEM_HEADS, MEM_HD))


def sample_layer(x, c_cmp, c_slc, c_sb, c_win, c_mem, page_table, rel_bias, p):
    B, S = x.shape[:2]
    assert S == 1, "the decode kernels handle one new token per sequence"
    q_n, kv_n, g_n, qkv_s, q_m, g_m = in_proj(x, p['g_attn'], p['w_in_segs'])
    win_buf = jnp.concatenate([c_win, kv_n[:, :, 4:6]], axis=1)
    n_phys = c_cmp.shape[0]
    hpp = PAGE_SIZE // CMP_STRIDE
    pages_per_step = next(t for t in (64, 32, 16, 8, 4, 2, 1) if n_phys % t == 0)
    h_all = halves_matmul(c_cmp.reshape(n_phys, PAGE_SIZE, -1), p['cmp_wbd'],
                          pages_per_step).reshape(n_phys, hpp, -1)
    o_n = nsa_decode_pallas(q_n[:, 0], g_n[:, 0], kv_n[:, 0], c_win, h_all, c_slc, page_table, rel_bias,
                            p['cmp_w2bd'], p['cmp_pe_term'])
    o_s, o_m = sb_mem_decode_pallas(qkv_s[:, 0, 0], q_m[:, 0], c_sb, c_mem, page_table)
    post = post_attn_pallas(x.reshape(1, B, D_MODEL), o_n[None], o_s[None], o_m[None], g_m.reshape(1, B, -1),
                            None, p, with_mem=False)
    post = [a.reshape(B, S, -1) for a in post]
    return post, (kv_n[:, :, 0:2], kv_n[:, :, 2:4], qkv_s[:, :, 1:3], win_buf[:, S:])


def kernel(x_prompt, x_sample, mem_prompt, cache_nsa_cmp, cache_nsa_slc, cache_sb, cache_nsa_win,
           cache_mem, page_table, rel_bias, g_attn, w_in, cmp_pe_k, cmp_w1_k, cmp_w2_k, cmp_pe_v,
           cmp_w1_v, cmp_w2_v, g_mem, w_mem_kv, w_br_nsa, w_br_sb, w_br_mem, w_out, g_ffn, w_router,
           b_router, w_up, b_up, w_down, b_down, g_final):
    depth = g_attn.shape[0]
    xp, xs = x_prompt, x_sample
    st_p, st_s = [], []
    for l in range(depth):
        p = {'g_attn': g_attn[l], 'w_in_segs': split_w_in(w_in[l]),
             'cmp_pe_k': cmp_pe_k[l], 'cmp_w1_k': cmp_w1_k[l], 'cmp_w2_k': cmp_w2_k[l],
             'cmp_pe_v': cmp_pe_v[l], 'cmp_w1_v': cmp_w1_v[l], 'cmp_w2_v': cmp_w2_v[l],
             'g_mem': g_mem[l], 'w_mem_kv': w_mem_kv[l],
             'w_br_nsa': w_br_nsa[l], 'w_br_sb': w_br_sb[l], 'w_br_mem': w_br_mem[l], 'w_out': w_out[l],
             'g_ffn': g_ffn[l], 'w_router': w_router[l], 'b_router': b_router[l],
             'w_up': w_up[l], 'b_up': b_up[l], 'w_down': w_down[l], 'b_down': b_down[l]}
        p['cmp_wbd'], p['cmp_w2bd'], p['cmp_pe_term'] = compress_weights(p, jnp.bfloat16)
        xp, sp = prompt_layer(xp, mem_prompt, rel_bias, p)
        xs, ss = sample_layer(xs, cache_nsa_cmp[l], cache_nsa_slc[l], cache_sb[l], cache_nsa_win[l],
                              cache_mem[l], page_table, rel_bias, p)
        xp = ffn_block(*xp, p, g_final, last=(l == depth - 1))
        xs = ffn_block(*xs, p, g_final, last=(l == depth - 1))
        st_p.append(sp)
        st_s.append(ss)
    y_prompt, y_sample = xp, xs

    def stacked(states, i):
        return jnp.stack([s[i] for s in states])

    return (y_prompt, y_sample,
            stacked(st_p, 0), stacked(st_p, 1), stacked(st_p, 2), stacked(st_p, 3), stacked(st_p, 4),
            stacked(st_s, 0), stacked(st_s, 1), stacked(st_s, 2), stacked(st_s, 3))
```

```python
import functools
import math

import jax
import jax.numpy as jnp
import numpy as np
from jax import lax
from jax.experimental import pallas as pl
from jax.experimental.pallas import tpu as pltpu

D_MODEL = 1024
PAGE_SIZE = 128
NSA_HEADS = 8
NSA_KV = 2
NSA_HPG = NSA_HEADS // NSA_KV
HEAD_DIM = 64
CMP_LEN = 32
CMP_STRIDE = 16
CMP_HID = 128
SLC_LEN = 64
SLC_TOP = 16
WINDOW = 512
NSA_QBLK = 32
FORCE_SCORE = 1.0e6
SB_HEADS = 8
SB_QBLK = 128
MEM_HEADS = 4
MEM_HD = 128
REL_BUCKETS = 32
REL_MAX_DIST = 2048
N_EXPERTS = 32
TOP_K = 4
D_FF = 1024
SWIGLU_LIMIT = 7.0
SWIGLU_ALPHA = 1.702
MOE_BLOCK = 256
EPS = 1e-6
NEG_INF = -1.0e30

SEG_QN = NSA_HEADS * HEAD_DIM
SEG_KVN = 6 * NSA_KV * HEAD_DIM
SEG_GN = 3 * NSA_HEADS
SEG_QKVS = 3 * SB_HEADS * HEAD_DIM
SEG_QM = MEM_HEADS * MEM_HD
SEG_GM = 3 * D_MODEL
IN_SIZES = (SEG_QN, SEG_KVN, SEG_GN, SEG_QKVS, SEG_QM, SEG_GM)

VMEM_LIMIT = 48 * 1024 * 1024


def _rms_proj_kernel(x_ref, g_ref, *refs, n_seg, sigmoid):
    w_refs = refs[:n_seg]
    o_refs = refs[n_seg:]
    x = x_ref[...]
    h = x * lax.rsqrt(jnp.mean(x * x, axis=-1, keepdims=True) + EPS) * g_ref[...]
    hb = h.astype(jnp.bfloat16)
    for w_ref, o_ref, sig in zip(w_refs, o_refs, sigmoid):
        z = jnp.dot(hb, w_ref[...], preferred_element_type=jnp.float32)
        if sig:
            z = jax.nn.sigmoid(z)
        o_ref[...] = z.astype(o_ref.dtype)


def rms_proj(x2d, g, weights, sigmoid, out_dtypes, tm):
    n, d = x2d.shape
    assert n % tm == 0
    n_seg = len(weights)
    in_specs = [pl.BlockSpec((tm, d), lambda i: (i, 0)),
                pl.BlockSpec((1, d), lambda i: (0, 0))]
    for w in weights:
        in_specs.append(pl.BlockSpec(w.shape, lambda i: (0, 0), pipeline_mode=pl.Buffered(1)))
    out_specs = [pl.BlockSpec((tm, w.shape[1]), lambda i: (i, 0)) for w in weights]
    out_shape = [jax.ShapeDtypeStruct((n, w.shape[1]), dt) for w, dt in zip(weights, out_dtypes)]
    return pl.pallas_call(
        functools.partial(_rms_proj_kernel, n_seg=n_seg, sigmoid=tuple(sigmoid)),
        grid=(n // tm,),
        in_specs=in_specs,
        out_specs=out_specs,
        out_shape=out_shape,
        compiler_params=pltpu.CompilerParams(dimension_semantics=("parallel",),
                                             vmem_limit_bytes=VMEM_LIMIT),
        name="rms_proj",
    )(x2d, g.reshape(1, d), *weights)


def in_proj(x, g_attn, w_segs):
    B, T, D = x.shape
    n = B * T
    tm = 256 if n % 256 == 0 else n
    outs = rms_proj(x.reshape(n, D), g_attn, w_segs,
                    sigmoid=(False, False, True, False, False, True),
                    out_dtypes=(jnp.float32,) * 6, tm=tm)
    q_n, kv_n, g_n, qkv_s, q_m, g_m = outs
    return (q_n.reshape(B, T, NSA_KV, NSA_HPG, HEAD_DIM),
            kv_n.reshape(B, T, 6, NSA_KV, HEAD_DIM),
            g_n.reshape(B, T, NSA_KV, NSA_HPG, 3),
            qkv_s.reshape(B, T, 3, SB_HEADS, HEAD_DIM),
            q_m.reshape(B, T, MEM_HEADS, MEM_HD),
            g_m.reshape(B, T, 3, D_MODEL))


def split_w_in(w_in):
    offs = np.cumsum((0,) + IN_SIZES)
    return [w_in[:, int(a):int(b)].astype(jnp.bfloat16) for a, b in zip(offs[:-1], offs[1:])]


def _nt_dot(a, b):
    return lax.dot_general(a, b, (((1,), (1,)), ((), ())), preferred_element_type=jnp.float32)


def _online_step(s, mask, v_aug, m, acc, cdt):
    h, tq, tk = s.shape
    s = jnp.where(mask[None], s, NEG_INF)
    m_new = jnp.maximum(m, jnp.max(s, axis=-1, keepdims=True))
    p = jnp.where(mask[None], jnp.exp(s - m_new), 0.0)
    alpha = jnp.exp(m - m_new).reshape(h * tq, 1)
    pv = jnp.dot(p.reshape(h * tq, tk).astype(cdt), v_aug, preferred_element_type=jnp.float32)
    return m_new, alpha * acc + pv


def _finish(acc):
    l = acc[:, HEAD_DIM:HEAD_DIM + 1]
    return acc[:, :HEAD_DIM] / jnp.where(l > 0.0, l, 1.0)


def _nsa_prompt_kernel(q_ref, gate_ref, kc_ref, vc_ref, ks_ref, vs_ref, kw_ref, vw_ref, cb_ref, tb_ref,
                       o_ref, *, tq, n_cmp, ns, cdt):
    f32 = jnp.float32
    qi = pl.program_id(2)
    q0 = qi * tq
    tk = tq
    hpg = NSA_HPG
    rows = hpg * tq
    q = q_ref[0].reshape(rows, HEAD_DIM)

    kc = kc_ref[0, 0]
    ncp = kc.shape[0]
    qpos_c = q0 + lax.broadcasted_iota(jnp.int32, (tq, ncp), 0)
    cidx = lax.broadcasted_iota(jnp.int32, (tq, ncp), 1)
    cmask = (cidx * CMP_STRIDE + (CMP_LEN - 1) <= qpos_c) & (cidx < n_cmp)
    s_c = (_nt_dot(q, kc) + cb_ref[0, 0]).reshape(hpg, tq, ncp)
    s_c = jnp.where(cmask[None], s_c, NEG_INF)
    m_c = jnp.max(s_c, axis=-1, keepdims=True)
    p_c = jnp.where(cmask[None], jnp.exp(s_c - m_c), 0.0)
    l_c = jnp.sum(p_c, axis=-1, keepdims=True)
    p_c = p_c / jnp.where(l_c > 0.0, l_c, 1.0)
    o_c = jnp.dot(p_c.reshape(rows, ncp).astype(cdt), vc_ref[0, 0], preferred_element_type=f32)

    p_sum = p_c[0]
    for h in range(1, hpg):
        p_sum = p_sum + p_c[h]
    jb = lax.broadcasted_iota(jnp.int32, (ns, ncp), 0)
    cc = lax.broadcasted_iota(jnp.int32, (ns, ncp), 1)
    ov_t = ((cc * CMP_STRIDE < jb * SLC_LEN + SLC_LEN)
            & (cc * CMP_STRIDE + (CMP_LEN - 1) >= jb * SLC_LEN)).astype(cdt)
    p_hi = p_sum.astype(cdt)
    p_lo = (p_sum - p_hi.astype(f32)).astype(cdt)
    imp = _nt_dot(ov_t, p_hi) + _nt_dot(ov_t, p_lo)
    blk = lax.broadcasted_iota(jnp.int32, (ns, tq), 0)
    qp_t = q0 + lax.broadcasted_iota(jnp.int32, (ns, tq), 1)
    cur = qp_t // SLC_LEN
    forced = (blk == 0) | (blk == cur) | (blk == cur - 1)
    imp = jnp.where(forced, FORCE_SCORE, imp)
    imp = jnp.where(blk * SLC_LEN <= qp_t, imp, -1.0)
    cnt = jnp.zeros((ns, tq), f32)
    for i in range(ns):
        row = imp[i:i + 1, :]
        ahead = (row > imp) | ((row == imp) & (blk > i))
        cnt = cnt + jnp.where(ahead, 1.0, 0.0)
    sel_t = jnp.where((cnt < float(min(SLC_TOP, ns))) & (imp >= 0.0), 1.0, 0.0)
    sel = sel_t.T.astype(cdt)

    m0 = jnp.full((hpg, tq, 1), NEG_INF, f32)
    a0 = jnp.zeros((rows, 2 * HEAD_DIM), f32)

    tk2 = 2 * tk
    qpos2 = q0 + lax.broadcasted_iota(jnp.int32, (tq, tk2), 0)
    kcol2 = lax.broadcasted_iota(jnp.int32, (tq, tk2), 1)
    eb2 = lax.broadcasted_iota(jnp.int32, (ns, tk2), 0)
    ek2 = lax.broadcasted_iota(jnp.int32, (ns, tk2), 1)

    def sel_body(kp, carry):
        m, acc = carry
        k0 = pl.multiple_of(kp * tk2, tk2)
        d0 = qi - 2 * kp
        bias = jnp.concatenate([tb_ref[0, d0], tb_ref[0, jnp.maximum(d0 - 1, 0)]], axis=-1)
        s = (_nt_dot(q, ks_ref[0, 0, pl.ds(k0, tk2), :]) + bias).reshape(hpg, tq, tk2)
        expand = jnp.where(eb2 == (k0 + ek2) // SLC_LEN, 1.0, 0.0).astype(cdt)
        picked = jnp.dot(sel, expand, preferred_element_type=f32)
        mask = (picked > 0.5) & (k0 + kcol2 <= qpos2)
        return _online_step(s, mask, vs_ref[0, 0, pl.ds(k0, tk2), :], m, acc, cdt)

    _, acc_s = lax.fori_loop(0, qi // 2 + 1, sel_body, (m0, a0))
    o_s = _finish(acc_s)

    first_w = jnp.maximum(qi - (WINDOW // tk + 1), 0)

    def win_body(j, carry):
        m, acc = carry
        tile = first_w + 2 * j
        k0 = pl.multiple_of(tile * tk, tk)
        d0 = qi - tile
        bias = jnp.concatenate([tb_ref[0, d0], tb_ref[0, jnp.maximum(d0 - 1, 0)]], axis=-1)
        s = (_nt_dot(q, kw_ref[0, 0, pl.ds(k0, tk2), :]) + bias).reshape(hpg, tq, tk2)
        dist = qpos2 - (k0 + kcol2)
        mask = (dist >= 0) & (dist < WINDOW)
        return _online_step(s, mask, vw_ref[0, 0, pl.ds(k0, tk2), :], m, acc, cdt)

    _, acc_w = lax.fori_loop(0, (qi - first_w) // 2 + 1, win_body, (m0, a0))
    o_w = _finish(acc_w)

    gates = gate_ref[0, 0]
    outs = []
    for h in range(hpg):
        r = slice(h * tq, (h + 1) * tq)
        outs.append(gates[:, 3 * h:3 * h + 1] * o_c[r]
                    + gates[:, 3 * h + 1:3 * h + 2] * o_s[r]
                    + gates[:, 3 * h + 2:3 * h + 3] * o_w[r])
    o_ref[0] = jnp.concatenate(outs, axis=-1).astype(o_ref.dtype)


def _aug_ones(v):
    return jnp.concatenate([v, jnp.ones_like(v)], axis=-1)


def nsa_bias_tables(rel_bias, t_len, tq):
    nq = t_len // tq
    ncp = t_len // CMP_STRIDE
    G, H = NSA_KV, NSA_HPG
    bias_d = rel_bias[rel_bucket(jnp.arange(t_len, dtype=jnp.int32))]
    bias_d = bias_d.reshape(t_len, G, H).transpose(1, 2, 0)
    q0 = jnp.arange(nq)[:, None] * tq

    def skew(rows, n):
        length = rows.shape[-1]
        lead = rows.shape[:-1]
        flat = jnp.broadcast_to(rows[..., None, :], lead + (n, length)).reshape(lead + (n * length,))
        return flat[..., :n * (length - 1)].reshape(lead + (n, length - 1))

    l = jnp.arange(2 * tq + 1)[None, :]
    w = bias_d[:, :, jnp.clip(q0 + (tq - 1) - l, 0, t_len - 1)]
    tb = skew(w.transpose(0, 2, 1, 3), tq)[..., tq - 1:2 * tq - 1]
    tb = tb.reshape(G, nq, H * tq, tq)
    A = tq // CMP_STRIDE
    b = jnp.arange(CMP_STRIDE)[None, :, None]
    mm = jnp.arange(ncp + A)[None, None, :]
    d_c = q0[:, :, None] + b - ((mm - (A - 1)) * CMP_STRIDE + (CMP_LEN - 1))
    f = bias_d[:, :, jnp.clip(d_c, 0, t_len - 1)]
    cb = skew(f.transpose(0, 2, 1, 3, 4), A)[..., A - 1:A - 1 + ncp]
    cb = cb.transpose(0, 1, 2, 4, 3, 5).reshape(G, nq, H * tq, ncp)
    return tb, cb


def nsa_prompt_pallas(q, gates, kc, vc, k_slc, v_slc, k_win, v_win, rel_bias, *, tq=128, cdt=jnp.bfloat16):
    B, T = q.shape[:2]
    n_cmp = kc.shape[1]
    ncp = T // CMP_STRIDE
    ns = T // SLC_LEN
    nq = T // tq
    assert nq % 2 == 0, "the selected-branch sweep reads key tiles in pairs"
    rows = NSA_HPG * tq
    tb, cb = nsa_bias_tables(rel_bias, T, tq)
    to_g = lambda a: a.astype(cdt).transpose(0, 2, 1, 3)
    qh = q.astype(cdt).reshape(B, T, NSA_HEADS, HEAD_DIM).transpose(0, 2, 1, 3)
    gt = gates.astype(jnp.float32).reshape(B, T, NSA_KV, NSA_HPG * 3).transpose(0, 2, 1, 3)
    pad_c = ((0, 0), (0, ncp - n_cmp), (0, 0), (0, 0))
    kcg = to_g(jnp.pad(kc, pad_c))
    vcg = to_g(jnp.pad(vc, pad_c))
    ksg, kwg = to_g(k_slc), to_g(k_win)
    vsg, vwg = _aug_ones(to_g(v_slc)), _aug_ones(to_g(v_win))
    per_bg = lambda w: pl.BlockSpec((1, 1, T, w), lambda g, b, i: (b, g, 0, 0))
    per_bg_c = lambda w: pl.BlockSpec((1, 1, ncp, w), lambda g, b, i: (b, g, 0, 0))
    return pl.pallas_call(
        functools.partial(_nsa_prompt_kernel, tq=tq, n_cmp=n_cmp, ns=ns, cdt=cdt),
        grid=(NSA_KV, B, nq),
        in_specs=[
            pl.BlockSpec((1, NSA_HPG, tq, HEAD_DIM), lambda g, b, i: (b, g, i, 0)),
            pl.BlockSpec((1, 1, tq, NSA_HPG * 3), lambda g, b, i: (b, g, i, 0)),
            per_bg_c(HEAD_DIM), per_bg_c(HEAD_DIM),
            per_bg(HEAD_DIM), per_bg(2 * HEAD_DIM), per_bg(HEAD_DIM), per_bg(2 * HEAD_DIM),
            pl.BlockSpec((1, 1, rows, ncp), lambda g, b, i: (g, i, 0, 0)),
            pl.BlockSpec((1, nq, rows, tq), lambda g, b, i: (g, 0, 0, 0), pipeline_mode=pl.Buffered(1)),
        ],
        out_specs=pl.BlockSpec((1, tq, NSA_HPG * HEAD_DIM), lambda g, b, i: (b, i, g)),
        out_shape=jax.ShapeDtypeStruct((B, T, NSA_HEADS * HEAD_DIM), cdt),
        compiler_params=pltpu.CompilerParams(dimension_semantics=("parallel", "parallel", "parallel"),
                                             vmem_limit_bytes=VMEM_LIMIT),
        name="nsa_prompt",
    )(qh, gt, kcg, vcg, ksg, vsg, kwg, vwg, cb, tb)


SB_DEAD_LOG = -104.0


def _split3(x, cdt):
    hi = x.astype(cdt)
    r = x - hi.astype(jnp.float32)
    mid = r.astype(cdt)
    lo = (r - mid.astype(jnp.float32)).astype(cdt)
    return hi, mid, lo


def _sb_tile(z, causal, carry, v, tri, cdt):
    h, tq, tk = z.shape
    sp = jnp.maximum(z, 0.0) + jnp.log1p(jnp.exp(-jnp.abs(z)))
    l1m = jnp.where(causal[None], -sp, 0.0).reshape(h * tq, tk)
    suffix = None
    for part in _split3(l1m, cdt):
        d = jnp.dot(part, tri, preferred_element_type=jnp.float32)
        suffix = d if suffix is None else suffix + d
    surv = (suffix + carry).reshape(h, tq, tk)
    a = jnp.where(causal[None], jnp.exp(z - sp + surv), 0.0)
    pv = jnp.einsum('hqk,hkd->hqd', a.astype(cdt), v, preferred_element_type=jnp.float32)
    return pv, carry + suffix[:, 0:1] + l1m[:, 0:1]


def _sb_prompt_kernel(q_ref, k_ref, v_ref, o_ref, *, tq, hs, cdt):
    f32 = jnp.float32
    qi = pl.program_id(2)
    q0 = qi * tq
    tk = tq
    q = q_ref[0, 0]
    qpos = q0 + lax.broadcasted_iota(jnp.int32, (tq, tk), 0)
    kcol = lax.broadcasted_iota(jnp.int32, (tq, tk), 1)
    tri = jnp.where(lax.broadcasted_iota(jnp.int32, (tk, tk), 0) > lax.broadcasted_iota(jnp.int32, (tk, tk), 1),
                    1.0, 0.0).astype(cdt)

    def cond(state):
        kb, live, _, _ = state
        return (kb >= 0) & (live > SB_DEAD_LOG)

    def body(state):
        kb, _, carry, acc = state
        k0 = pl.multiple_of(kb * tk, tk)
        k = k_ref[0, 0, :, pl.ds(k0, tk), :]
        v = v_ref[0, 0, :, pl.ds(k0, tk), :]
        z = jnp.einsum('hqd,hkd->hqk', q, k, preferred_element_type=f32)
        pv, carry = _sb_tile(z, k0 + kcol < qpos, carry, v, tri, cdt)
        return kb - 1, jnp.max(carry), carry, acc + pv

    init = (qi, jnp.float32(0.0), jnp.zeros((hs * tq, 1), f32), jnp.zeros((hs, tq, HEAD_DIM), f32))
    _, _, _, acc = lax.while_loop(cond, body, init)
    o_ref[0] = jnp.concatenate([acc[h] for h in range(hs)], axis=-1).astype(o_ref.dtype)


def sb_prompt_pallas(qkv, *, tq=128, hs=4, cdt=jnp.bfloat16):
    B, T = qkv.shape[:2]
    x = qkv.astype(cdt).transpose(0, 2, 3, 1, 4)
    full = lambda j: pl.BlockSpec((1, 1, hs, T, HEAD_DIM), lambda b, g, i: (b, j, g, 0, 0))
    return pl.pallas_call(
        functools.partial(_sb_prompt_kernel, tq=tq, hs=hs, cdt=cdt),
        grid=(B, SB_HEADS // hs, T // tq),
        in_specs=[pl.BlockSpec((1, 1, hs, tq, HEAD_DIM), lambda b, g, i: (b, 0, g, i, 0)), full(1), full(2)],
        out_specs=pl.BlockSpec((1, tq, hs * HEAD_DIM), lambda b, g, i: (b, i, g)),
        out_shape=jax.ShapeDtypeStruct((B, T, SB_HEADS * HEAD_DIM), cdt),
        compiler_params=pltpu.CompilerParams(dimension_semantics=("parallel", "parallel", "parallel"),
                                             vmem_limit_bytes=VMEM_LIMIT),
        name="sb_prompt",
    )(x, x, x)


def _moe_kernel(blk_e_ref, n_used_ref, tok_ref, dst_ref, rw_ref, h_hbm, wup_ref, bup_ref, wdn_ref, bdn_ref,
                parts_hbm, xbuf, obuf, sem_in, sem_out, *, blk, cdt):
    j = pl.program_id(0)

    def gather_copy(r):
        return pltpu.make_async_copy(h_hbm.at[pl.ds(tok_ref[0, 0, r], 1)], xbuf.at[pl.ds(r, 1)], sem_in)

    def scatter_copy(r):
        return pltpu.make_async_copy(obuf.at[pl.ds(r, 1)], parts_hbm.at[pl.ds(dst_ref[0, 0, r], 1)], sem_out)

    def each_row(fn):
        def body(r, c):
            fn(r)
            return c
        lax.fori_loop(0, blk, body, 0, unroll=8)

    @pl.when(j < n_used_ref[0])
    def _():
        each_row(lambda r: gather_copy(r).start())
        each_row(lambda r: gather_copy(r).wait())
        x = xbuf[...].astype(cdt)
        gu = jnp.dot(x, wup_ref[0], preferred_element_type=jnp.float32) + bup_ref[0]
        gate = jnp.minimum(gu[:, :D_FF], SWIGLU_LIMIT)
        up = jnp.clip(gu[:, D_FF:], -SWIGLU_LIMIT, SWIGLU_LIMIT)
        act = (up + 1.0) * gate * jax.nn.sigmoid(SWIGLU_ALPHA * gate)
        out = jnp.dot(act.astype(cdt), wdn_ref[0], preferred_element_type=jnp.float32) + bdn_ref[0]
        obuf[...] = out * rw_ref[...]
        each_row(lambda r: scatter_copy(r).start())
        each_row(lambda r: scatter_copy(r).wait())


def moe_ffn_pallas(h_pad, blk_e, n_used, row_tok, row_dst, row_w, w_up, b_up, w_down, b_down, *, blk, n_assign,
                   cdt=jnp.bfloat16):
    n_rows = row_tok.shape[0]
    n_blocks = n_rows // blk
    d = h_pad.shape[1]
    smem_rows = lambda: pl.BlockSpec((1, 1, blk), lambda j, be, nu: (j, 0, 0), memory_space=pltpu.SMEM)
    grid_spec = pltpu.PrefetchScalarGridSpec(
        num_scalar_prefetch=2,
        grid=(n_blocks,),
        in_specs=[
            smem_rows(), smem_rows(),
            pl.BlockSpec((blk, 1), lambda j, be, nu: (j, 0)),
            pl.BlockSpec(memory_space=pl.ANY),
            pl.BlockSpec((1, d, 2 * D_FF), lambda j, be, nu: (be[j], 0, 0)),
            pl.BlockSpec((1, 1, 2 * D_FF), lambda j, be, nu: (be[j], 0, 0)),
            pl.BlockSpec((1, D_FF, d), lambda j, be, nu: (be[j], 0, 0)),
            pl.BlockSpec((1, 1, d), lambda j, be, nu: (be[j], 0, 0)),
        ],
        out_specs=pl.BlockSpec(memory_space=pl.ANY),
        scratch_shapes=[pltpu.VMEM((blk, d), jnp.float32), pltpu.VMEM((blk, d), jnp.float32),
                        pltpu.SemaphoreType.DMA(()), pltpu.SemaphoreType.DMA(())],
    )
    return pl.pallas_call(
        functools.partial(_moe_kernel, blk=blk, cdt=cdt),
        grid_spec=grid_spec,
        out_shape=jax.ShapeDtypeStruct((n_assign + blk, d), jnp.float32),
        compiler_params=pltpu.CompilerParams(dimension_semantics=("arbitrary",), vmem_limit_bytes=VMEM_LIMIT),
        name="moe_ffn",
    )(blk_e, n_used, row_tok.reshape(n_blocks, 1, blk), row_dst.reshape(n_blocks, 1, blk),
      row_w.reshape(n_rows, 1), h_pad, w_up.astype(cdt), b_up.reshape(N_EXPERTS, 1, 2 * D_FF),
      w_down.astype(cdt), b_down.reshape(N_EXPERTS, 1, d))


def _post_attn_kernel(x_ref, on_ref, os_ref, m_ref, gm_ref, memkv_ref, wn_ref, ws_ref, wm_ref, wo_ref, gf_ref,
                      wr_ref, br_ref, x1_ref, hf_ref, lg_ref, *, with_mem, cdt):
    f32 = jnp.float32
    if with_mem:
        qm = m_ref[0].astype(cdt)
        mem = memkv_ref[0]
        half = MEM_HEADS * MEM_HD
        heads = []
        for h in range(MEM_HEADS):
            c = slice(h * MEM_HD, (h + 1) * MEM_HD)
            s = _nt_dot(qm[:, c], mem[:, c]) * (MEM_HD ** -0.5)
            pm = jnp.exp(s - jnp.max(s, axis=-1, keepdims=True))
            pm = pm / jnp.sum(pm, axis=-1, keepdims=True)
            heads.append(jnp.dot(pm.astype(cdt), mem[:, half + h * MEM_HD:half + (h + 1) * MEM_HD],
                                 preferred_element_type=f32))
        o_m = jnp.concatenate(heads, axis=-1)
    else:
        o_m = m_ref[0]
    d = x_ref.shape[-1]
    gm = gm_ref[0]
    y = (gm[:, :d] * jnp.dot(on_ref[0].astype(cdt), wn_ref[...], preferred_element_type=f32)
         + gm[:, d:2 * d] * jnp.dot(os_ref[0].astype(cdt), ws_ref[...], preferred_element_type=f32)
         + gm[:, 2 * d:] * jnp.dot(o_m.astype(cdt), wm_ref[...], preferred_element_type=f32))
    x1 = x_ref[0] + jnp.dot(y.astype(cdt), wo_ref[...], preferred_element_type=f32)
    x1_ref[0] = x1
    hf = x1 * lax.rsqrt(jnp.mean(x1 * x1, axis=-1, keepdims=True) + EPS) * gf_ref[...]
    hf_ref[0] = hf
    lg_ref[0] = jnp.dot(hf.astype(cdt), wr_ref[...], preferred_element_type=f32) + br_ref[...]


def post_attn_pallas(x, o_n, o_s, m, g_m, mem_kv, p, *, with_mem, cdt=jnp.bfloat16):
    B, T, d = x.shape
    tm = 256 if T % 256 == 0 else T
    if mem_kv is None:
        mem_kv = jnp.zeros((B, 8, 2 * MEM_HEADS * MEM_HD), cdt)
    ml = mem_kv.shape[1]
    rows = lambda w: pl.BlockSpec((1, tm, w), lambda b, i: (b, i, 0))
    const = lambda a: pl.BlockSpec(a.shape, lambda b, i: (0,) * a.ndim)
    weights = [p['w_br_nsa'].astype(cdt), p['w_br_sb'].astype(cdt), p['w_br_mem'].astype(cdt),
               p['w_out'].astype(cdt), p['g_ffn'].reshape(1, d), p['w_router'].astype(cdt),
               p['b_router'].reshape(1, N_EXPERTS)]
    return pl.pallas_call(
        functools.partial(_post_attn_kernel, with_mem=with_mem, cdt=cdt),
        grid=(B, T // tm),
        in_specs=[rows(d), rows(o_n.shape[-1]), rows(o_s.shape[-1]), rows(m.shape[-1]), rows(3 * d),
                  pl.BlockSpec((1, ml, mem_kv.shape[-1]), lambda b, i: (b, 0, 0))] + [const(w) for w in weights],
        out_specs=[rows(d), rows(d), rows(N_EXPERTS)],
        out_shape=[jax.ShapeDtypeStruct((B, T, d), jnp.float32), jax.ShapeDtypeStruct((B, T, d), jnp.float32),
                   jax.ShapeDtypeStruct((B, T, N_EXPERTS), jnp.float32)],
        compiler_params=pltpu.CompilerParams(dimension_semantics=("parallel", "parallel"),
                                             vmem_limit_bytes=VMEM_LIMIT),
        name="post_attn",
    )(x, o_n, o_s, m, g_m, mem_kv.astype(cdt), *weights)


def _combine_norm_kernel(x_ref, *refs, normalize):
    part_refs, g_ref, o_ref = refs[:TOP_K], refs[TOP_K], refs[TOP_K + 1]
    x = x_ref[...]
    for part_ref in part_refs:
        x = x + part_ref[...]
    if normalize:
        x = x * lax.rsqrt(jnp.mean(x * x, axis=-1, keepdims=True) + EPS) * g_ref[...]
    o_ref[...] = x


def combine_norm_pallas(x, parts, g_final, tm, normalize):
    n, d = x.shape
    nt = n // tm
    part_spec = lambda k: pl.BlockSpec((tm, d), lambda i: (k * nt + i, 0))
    return pl.pallas_call(
        functools.partial(_combine_norm_kernel, normalize=normalize),
        grid=(nt,),
        in_specs=[pl.BlockSpec((tm, d), lambda i: (i, 0))] + [part_spec(k) for k in range(TOP_K)]
                 + [pl.BlockSpec((1, d), lambda i: (0, 0))],
        out_specs=pl.BlockSpec((tm, d), lambda i: (i, 0)),
        out_shape=jax.ShapeDtypeStruct((n, d), jnp.float32),
        compiler_params=pltpu.CompilerParams(dimension_semantics=("parallel",), vmem_limit_bytes=VMEM_LIMIT),
        name="combine_norm",
    )(x, *([parts] * TOP_K), g_final.reshape(1, d))


def ffn_block(x, hf, logits, p, g_final, last):
    lead = x.shape[:-1]
    x2 = x.reshape(-1, D_MODEL)
    hf = hf.reshape(-1, D_MODEL)
    logits = logits.reshape(-1, N_EXPERTS)
    N = x2.shape[0]
    top_l, top_e = lax.top_k(logits, TOP_K)
    top_w = jax.nn.softmax(top_l, axis=-1)
    n_assign = N * TOP_K
    blk = int(min(MOE_BLOCK, max(8, n_assign // N_EXPERTS)))
    n_rows = (n_assign // blk + N_EXPERTS) * blk
    n_blocks = n_rows // blk
    flat_e = top_e.reshape(-1)
    order = jnp.argsort(flat_e).astype(jnp.int32)
    se = flat_e[order]
    counts = jnp.bincount(flat_e, length=N_EXPERTS)
    start = jnp.cumsum(counts) - counts
    padded = (counts + blk - 1) // blk * blk
    p_end = jnp.cumsum(padded)
    p_start = p_end - padded
    dest = p_start[se] + jnp.arange(n_assign) - start[se]
    aid = jnp.full((n_rows,), n_assign, jnp.int32).at[dest].set(order)
    real = aid < n_assign
    row_tok = jnp.where(real, aid // TOP_K, N)
    row_dst = jnp.where(real, (aid % TOP_K) * N + aid // TOP_K, n_assign + jnp.arange(n_rows) % blk)
    row_w = jnp.where(real, top_w.reshape(-1)[jnp.minimum(aid, n_assign - 1)], 0.0)
    blk_e = jnp.minimum(jnp.searchsorted(p_end, jnp.arange(n_blocks) * blk, side='right'),
                        N_EXPERTS - 1).astype(jnp.int32)
    n_used = (p_end[-1] // blk).astype(jnp.int32).reshape(1)
    h_pad = jnp.concatenate([hf, jnp.zeros((1, D_MODEL), hf.dtype)], axis=0)
    parts = moe_ffn_pallas(h_pad, blk_e, n_used, row_tok, row_dst, row_w, p['w_up'], p['b_up'],
                           p['w_down'], p['b_down'], blk=blk, n_assign=n_assign)
    tm = 256 if N % 256 == 0 else N
    y = combine_norm_pallas(x2, parts, g_final, tm, normalize=last)
    return y.reshape(*lead, D_MODEL)


def _rows_matmul_kernel(x_ref, w_ref, o_ref):
    o_ref[...] = jnp.dot(x_ref[...].astype(w_ref.dtype), w_ref[...], preferred_element_type=jnp.float32)


def rows_matmul(x, w, tm):
    r, k = x.shape
    n = w.shape[1]
    return pl.pallas_call(
        _rows_matmul_kernel,
        grid=(r // tm,),
        in_specs=[pl.BlockSpec((tm, k), lambda i: (i, 0)),
                  pl.BlockSpec((k, n), lambda i: (0, 0), pipeline_mode=pl.Buffered(1))],
        out_specs=pl.BlockSpec((tm, n), lambda i: (i, 0)),
        out_shape=jax.ShapeDtypeStruct((r, n), jnp.float32),
        compiler_params=pltpu.CompilerParams(dimension_semantics=("parallel",), vmem_limit_bytes=VMEM_LIMIT),
        name="rows_matmul",
    )(x, w)


def _halves_matmul_kernel(*refs):
    x_refs, w_ref, o_ref = refs[:-2], refs[-2], refs[-1]
    n = o_ref.shape[0]
    pages, page = x_refs[0].shape[:2]
    strips = len(x_refs)
    acc = None
    for s in range(CMP_STRIDE):
        for c, x_ref in enumerate(x_refs):
            xs = x_ref[:, pl.ds(s, page // CMP_STRIDE, stride=CMP_STRIDE), :].reshape(n, x_ref.shape[2])
            d = jnp.dot(xs.astype(w_ref.dtype), w_ref[s * strips + c], preferred_element_type=jnp.float32)
            acc = d if acc is None else acc + d
    o_ref[...] = acc


def halves_matmul(x, wbd, pages_per_step):
    n_pg, page, width = x.shape
    n = wbd.shape[1]
    lane = 128
    strips = width // lane
    strip_spec = lambda c: pl.BlockSpec((pages_per_step, page, lane), lambda i: (i, 0, c))
    return pl.pallas_call(
        _halves_matmul_kernel,
        grid=(n_pg // pages_per_step,),
        in_specs=[strip_spec(c) for c in range(strips)]
                 + [pl.BlockSpec((CMP_STRIDE * strips, lane, n), lambda i: (0, 0, 0), pipeline_mode=pl.Buffered(1))],
        out_specs=pl.BlockSpec((pages_per_step * page // CMP_STRIDE, n), lambda i: (i, 0)),
        out_shape=jax.ShapeDtypeStruct((n_pg * page // CMP_STRIDE, n), jnp.float32),
        compiler_params=pltpu.CompilerParams(dimension_semantics=("parallel",), vmem_limit_bytes=VMEM_LIMIT),
        name="halves_matmul",
    )(*([x] * strips), wbd.reshape(CMP_STRIDE * strips, lane, n))


def compress_weights(p, cdt):
    R = CMP_LEN // CMP_STRIDE
    nc4 = 2 * NSA_KV
    w1 = jnp.stack([p['cmp_w1_k'], p['cmp_w1_k'], p['cmp_w1_v'], p['cmp_w1_v']])
    w1 = w1.reshape(nc4, R, CMP_STRIDE, HEAD_DIM, CMP_HID)
    eye = jnp.eye(nc4, dtype=w1.dtype)
    wbd = jnp.einsum('crsdh,ce->scdreh', w1, eye).reshape(CMP_STRIDE * nc4 * HEAD_DIM, R * nc4 * CMP_HID)
    w2 = jnp.stack([p['cmp_w2_k'], p['cmp_w2_k'], p['cmp_w2_v'], p['cmp_w2_v']])
    w2bd = jnp.einsum('chd,ce->ched', w2, eye).reshape(nc4 * CMP_HID, nc4 * HEAD_DIM)
    pe = jnp.stack([p['cmp_pe_k'].reshape(-1) @ p['cmp_w1_k'], p['cmp_pe_k'].reshape(-1) @ p['cmp_w1_k'],
                    p['cmp_pe_v'].reshape(-1) @ p['cmp_w1_v'], p['cmp_pe_v'].reshape(-1) @ p['cmp_w1_v']])
    return wbd.astype(cdt), w2bd.astype(cdt), pe.reshape(1, nc4 * CMP_HID)


def rows_to_columns(pages):
    return pages.transpose(0, 2, 1)


def _softmax_with_extra(s, mask, s_x):
    s = jnp.where(mask, s, NEG_INF)
    m = jnp.maximum(jnp.max(s, axis=-1, keepdims=True), s_x)
    p = jnp.where(mask, jnp.exp(s - m), 0.0)
    p_x = jnp.exp(s_x - m)
    inv = 1.0 / (jnp.sum(p, axis=-1, keepdims=True) + p_x)
    return p * inv, p_x * inv


def _nsa_decode_kernel(pt_ref, qk_ref, gate_ref, news_ref, neww_ref, win_ref, pe_ref, w2_ref, cbias_ref,
                       sbias_ref, wbias_ref, b0_ref, e_ref, h_hbm, slc_hbm, o_ref, hbuf, sbuf, sem,
                       *, n_pages, n_cmp, ns, cdt):
    f32 = jnp.float32
    b = pl.program_id(0)
    nb = pl.num_programs(0)
    slot = b % 2
    hpp = PAGE_SIZE // CMP_STRIDE

    def copies(bb, sl):
        out = []
        for p in range(n_pages):
            pg = pt_ref[bb, p]
            out.append(pltpu.make_async_copy(h_hbm.at[pg], hbuf.at[sl, pl.ds(p * hpp, hpp)], sem.at[0, sl]))
            out.append(pltpu.make_async_copy(slc_hbm.at[pg], sbuf.at[sl, :, pl.ds(p * PAGE_SIZE, PAGE_SIZE)],
                                             sem.at[1, sl]))
        return out

    @pl.when(b == 0)
    def _():
        for c in copies(0, 0):
            c.start()

    @pl.when(b + 1 < nb)
    def _():
        for c in copies(b + 1, 1 - slot):
            c.start()

    for c in copies(b, slot):
        c.wait()

    qk = qk_ref[0]
    nh = qk.shape[0]
    width = qk.shape[1]
    row = lax.broadcasted_iota(jnp.int32, (nh, 1), 0)

    def own_values(o_full):
        v0 = o_full[:, NSA_KV * HEAD_DIM:NSA_KV * HEAD_DIM + HEAD_DIM]
        v1 = o_full[:, NSA_KV * HEAD_DIM + HEAD_DIM:]
        return jnp.where(row < NSA_HPG, v0, v1)

    hh = hbuf[slot]
    ncp = hh.shape[0]
    half = hh.shape[1] // 2
    pre = hh[:, :half] + pltpu.roll(hh[:, half:], ncp - 1, 0) + pe_ref[...]
    kvc = jnp.dot(jax.nn.gelu(pre).astype(cdt), w2_ref[...], preferred_element_type=f32).astype(cdt)
    cidx = lax.broadcasted_iota(jnp.int32, (nh, ncp), 1)
    cmask = cidx < n_cmp
    s_c = jnp.where(cmask, _nt_dot(qk, kvc) + cbias_ref[...], NEG_INF)
    m_c = jnp.max(s_c, axis=-1, keepdims=True)
    p_c = jnp.where(cmask, jnp.exp(s_c - m_c), 0.0)
    p_c = p_c / jnp.sum(p_c, axis=-1, keepdims=True)
    o_c = own_values(jnp.dot(p_c.astype(cdt), kvc, preferred_element_type=f32))

    sum0 = jnp.sum(jnp.where(row < NSA_HPG, p_c, 0.0), axis=0, keepdims=True)
    sum1 = jnp.sum(jnp.where(row < NSA_HPG, 0.0, p_c), axis=0, keepdims=True)
    p_sum = jnp.where(row < NSA_HPG, sum0, sum1)
    nsp = e_ref.shape[0]
    cc = lax.broadcasted_iota(jnp.int32, (ncp, nsp), 0)
    jb = lax.broadcasted_iota(jnp.int32, (ncp, nsp), 1)
    ov = ((cc * CMP_STRIDE < jb * SLC_LEN + SLC_LEN)
          & (cc * CMP_STRIDE + (CMP_LEN - 1) >= jb * SLC_LEN)).astype(cdt)
    p_hi = p_sum.astype(cdt)
    p_lo = (p_sum - p_hi.astype(f32)).astype(cdt)
    imp = jnp.dot(p_hi, ov, preferred_element_type=f32) + jnp.dot(p_lo, ov, preferred_element_type=f32)
    past = n_pages * PAGE_SIZE
    blk = lax.broadcasted_iota(jnp.int32, (nh, nsp), 1)
    cur = past // SLC_LEN
    forced = (blk == 0) | (blk == cur) | (blk == cur - 1)
    imp = jnp.where(forced, FORCE_SCORE, imp)
    imp = jnp.where(blk < ns, imp, -1.0)
    imp_t = imp.T
    bi = lax.broadcasted_iota(jnp.int32, (nsp, nsp), 0)
    bj = lax.broadcasted_iota(jnp.int32, (nsp, nsp), 1)
    sel_rows = []
    for g in range(NSA_KV):
        col = imp_t[:, g * NSA_HPG:g * NSA_HPG + 1]
        rw = imp[g * NSA_HPG:g * NSA_HPG + 1, :]
        ahead = (col > rw) | ((col == rw) & (bi < bj))
        cnt = jnp.sum(jnp.where(ahead, 1.0, 0.0), axis=0, keepdims=True)
        sel_rows.append(jnp.where((cnt < float(min(SLC_TOP, ns))) & (rw >= 0.0), 1.0, 0.0))
    sel = jnp.where(row < NSA_HPG, sel_rows[0], sel_rows[1])

    keys_t = sbuf[slot].astype(cdt)
    picked = jnp.dot(sel.astype(cdt), e_ref[...], preferred_element_type=f32)
    new_s = news_ref[0]
    s_x = jnp.sum(qk.astype(f32) * new_s.astype(cdt).astype(f32), axis=-1, keepdims=True) + b0_ref[...]
    p_s, p_x = _softmax_with_extra(jnp.dot(qk, keys_t, preferred_element_type=f32) + sbias_ref[...],
                                   picked > 0.5, s_x)
    o_s = own_values(_nt_dot(p_s.astype(cdt), keys_t)
                     + p_x.astype(cdt).astype(f32) * new_s.astype(cdt).astype(f32))

    wrows_t = win_ref[0].astype(cdt)
    wb = wrows_t.shape[1]
    new_w = neww_ref[0]
    widx = lax.broadcasted_iota(jnp.int32, (nh, wb), 1)
    s_xw = jnp.sum(qk.astype(f32) * new_w.astype(cdt).astype(f32), axis=-1, keepdims=True) + b0_ref[...]
    p_w, p_xw = _softmax_with_extra(jnp.dot(qk, wrows_t, preferred_element_type=f32) + wbias_ref[...],
                                    wb - widx < WINDOW, s_xw)
    o_w = own_values(_nt_dot(p_w.astype(cdt), wrows_t)
                     + p_xw.astype(cdt).astype(f32) * new_w.astype(cdt).astype(f32))

    gates = gate_ref[0]
    o_ref[0] = gates[:, 0:1] * o_c + gates[:, 1:2] * o_s + gates[:, 2:3] * o_w


def nsa_decode_pallas(q_n, gates, kv_new, c_win, h_all, c_slc, page_table, rel_bias, w2bd, pe_term,
                      cdt=jnp.bfloat16):
    B = q_n.shape[0]
    n_pages = page_table.shape[1]
    past = n_pages * PAGE_SIZE
    wb = c_win.shape[1]
    hpp = PAGE_SIZE // CMP_STRIDE
    ncp = n_pages * hpp
    n_cmp = ncp - (CMP_LEN // CMP_STRIDE - 1)
    ns = -(-(past + 1) // SLC_LEN)
    nsp = -(-ns // 128) * 128
    width = 2 * NSA_KV * HEAD_DIM
    H = NSA_HEADS
    qs = (q_n * HEAD_DIM ** -0.5).reshape(B, NSA_KV, NSA_HPG, HEAD_DIM)
    qk = jnp.einsum('bghd,ge->bghed', qs, jnp.eye(NSA_KV, dtype=qs.dtype)).reshape(B, H, NSA_KV * HEAD_DIM)
    qk = jnp.pad(qk, ((0, 0), (0, 0), (0, width - NSA_KV * HEAD_DIM))).astype(cdt)
    bias_d = rel_bias[rel_bucket(jnp.arange(past + 1, dtype=jnp.int32))].T
    c_end = jnp.arange(ncp) * CMP_STRIDE + (CMP_LEN - 1)
    cbias = bias_d[:, jnp.clip(past - c_end, 0, past)]
    sbias = bias_d[:, past - jnp.arange(past)]
    wbias = bias_d[:, jnp.clip(wb - jnp.arange(wb), 0, past)]
    b0 = bias_d[:, 0:1]
    expand = (jnp.arange(nsp)[:, None] == (jnp.arange(past) // SLC_LEN)[None, :]).astype(cdt)
    const = lambda shape: pl.BlockSpec(shape, lambda b, pt: (0,) * len(shape))
    per_b = lambda shape: pl.BlockSpec((1,) + shape, lambda b, pt: (b,) + (0,) * len(shape))
    grid_spec = pltpu.PrefetchScalarGridSpec(
        num_scalar_prefetch=1,
        grid=(B,),
        in_specs=[per_b((H, width)), per_b((H, 3)), per_b((1, width)), per_b((1, width)), per_b((width, wb)),
                  const((1, pe_term.shape[1])), const(w2bd.shape), const((H, ncp)), const((H, past)),
                  const((H, wb)), const((H, 1)),
                  pl.BlockSpec((nsp, past), lambda b, pt: (0, 0), pipeline_mode=pl.Buffered(1)),
                  pl.BlockSpec(memory_space=pl.ANY), pl.BlockSpec(memory_space=pl.ANY)],
        out_specs=per_b((H, HEAD_DIM)),
        scratch_shapes=[pltpu.VMEM((2, ncp, h_all.shape[2]), jnp.float32),
                        pltpu.VMEM((2, width, past), jnp.float32),
                        pltpu.SemaphoreType.DMA((2, 2))],
    )
    out = pl.pallas_call(
        functools.partial(_nsa_decode_kernel, n_pages=n_pages, n_cmp=n_cmp, ns=ns, cdt=cdt),
        grid_spec=grid_spec,
        out_shape=jax.ShapeDtypeStruct((B, H, HEAD_DIM), jnp.float32),
        compiler_params=pltpu.CompilerParams(dimension_semantics=("arbitrary",), vmem_limit_bytes=VMEM_LIMIT),
        name="nsa_decode",
    )(page_table, qk, gates.reshape(B, H, 3), kv_new[:, 2:4].reshape(B, 1, width),
      kv_new[:, 4:6].reshape(B, 1, width), rows_to_columns(c_win.reshape(B, wb, width)), pe_term, w2bd, cbias,
      sbias, wbias, b0, expand, h_all, rows_to_columns(c_slc.reshape(c_slc.shape[0], PAGE_SIZE, width)))
    return out.reshape(B, H * HEAD_DIM)


def _compress_finish_kernel(h_ref, pe_ref, w2_ref, o_ref):
    hh = h_ref[0]
    nh = hh.shape[0]
    half = hh.shape[1] // 2
    pre = hh[:, :half] + pltpu.roll(hh[:, half:], nh - 1, 0) + pe_ref[...]
    o_ref[0] = jnp.dot(jax.nn.gelu(pre).astype(w2_ref.dtype), w2_ref[...], preferred_element_type=jnp.float32)


def compress_prompt_pallas(cmp_rows, wbd, w2bd, pe_term):
    B, T = cmp_rows.shape[:2]
    nh = T // CMP_STRIDE
    width = 2 * NSA_KV * HEAD_DIM
    rows = B * nh
    tm = next(t for t in (512, 256, 128, 64, 32, 16, 8) if rows % t == 0)
    h = rows_matmul(cmp_rows.reshape(rows, CMP_STRIDE * width), wbd, tm).reshape(B, nh, wbd.shape[1])
    out = pl.pallas_call(
        _compress_finish_kernel,
        grid=(B,),
        in_specs=[pl.BlockSpec((1, nh, wbd.shape[1]), lambda b: (b, 0, 0)),
                  pl.BlockSpec(pe_term.shape, lambda b: (0, 0)),
                  pl.BlockSpec(w2bd.shape, lambda b: (0, 0))],
        out_specs=pl.BlockSpec((1, nh, width), lambda b: (b, 0, 0)),
        out_shape=jax.ShapeDtypeStruct((B, nh, width), jnp.float32),
        compiler_params=pltpu.CompilerParams(dimension_semantics=("parallel",), vmem_limit_bytes=VMEM_LIMIT),
        name="compress_finish",
    )(h, pe_term, w2bd)
    return out.reshape(B, nh, 2, NSA_KV, HEAD_DIM)


SB_CHUNK_PAGES = 2


def _sb_mem_decode_kernel(pt_ref, qs_ref, qm_ref, mem_ref, sb_hbm, os_ref, om_ref, first, extra, sem,
                          *, n_pages, cdt):
    f32 = jnp.float32
    b = pl.program_id(0)
    nb = pl.num_programs(0)
    slot = b % 2
    ch = SB_CHUNK_PAGES
    ck = ch * PAGE_SIZE
    n_chunks = n_pages // ch

    def copies(bb, c, dst, s):
        return [pltpu.make_async_copy(sb_hbm.at[pt_ref[bb, n_pages - (c + 1) * ch + i]],
                                      dst.at[:, pl.ds(i * PAGE_SIZE, PAGE_SIZE)], s) for i in range(ch)]

    @pl.when(b == 0)
    def _():
        for c in copies(0, 0, first.at[0], sem.at[0]):
            c.start()

    @pl.when(b + 1 < nb)
    def _():
        for c in copies(b + 1, 0, first.at[1 - slot], sem.at[1 - slot]):
            c.start()

    for c in copies(b, 0, first.at[slot], sem.at[slot]):
        c.wait()

    qs = qs_ref[0]
    nh = qs.shape[0]
    tri = jnp.where(lax.broadcasted_iota(jnp.int32, (ck, ck), 0) > lax.broadcasted_iota(jnp.int32, (ck, ck), 1),
                    1.0, 0.0).astype(cdt)

    def chunk(rows, carry, acc):
        keys_t = rows.astype(cdt)
        z = jnp.dot(qs, keys_t, preferred_element_type=f32)
        sp = jnp.maximum(z, 0.0) + jnp.log1p(jnp.exp(-jnp.abs(z)))
        l1m = -sp
        suffix = None
        for part in _split3(l1m, cdt):
            d = jnp.dot(part, tri, preferred_element_type=f32)
            suffix = d if suffix is None else suffix + d
        a = jnp.exp(z - sp + suffix + carry)
        acc = acc + _nt_dot(a.astype(cdt), keys_t)
        return carry + suffix[:, 0:1] + l1m[:, 0:1], acc

    carry, acc = chunk(first[slot], jnp.zeros((nh, 1), f32), jnp.zeros((nh, qs.shape[1]), f32))

    def cond(state):
        c, live, _, _ = state
        return (c < n_chunks) & (live > SB_DEAD_LOG)

    def body(state):
        c, _, carry, acc = state
        for cp in copies(b, c, extra, sem.at[2]):
            cp.start()
        for cp in copies(b, c, extra, sem.at[2]):
            cp.wait()
        carry, acc = chunk(extra[...], carry, acc)
        return c + 1, jnp.max(carry), carry, acc

    _, _, _, acc = lax.while_loop(cond, body, (jnp.int32(1), jnp.max(carry), carry, acc))
    v0 = SB_HEADS * HEAD_DIM
    os_ref[0] = jnp.concatenate([acc[h:h + 1, v0 + h * HEAD_DIM:v0 + (h + 1) * HEAD_DIM] for h in range(nh)], axis=0)

    mem = mem_ref[0].astype(cdt)
    s = _nt_dot(qm_ref[0], mem) * (MEM_HD ** -0.5)
    p = jnp.exp(s - jnp.max(s, axis=-1, keepdims=True))
    p = p / jnp.sum(p, axis=-1, keepdims=True)
    om = jnp.dot(p.astype(cdt), mem, preferred_element_type=f32)
    m0 = MEM_HEADS * MEM_HD
    om_ref[0] = jnp.concatenate([om[h:h + 1, m0 + h * MEM_HD:m0 + (h + 1) * MEM_HD] for h in range(MEM_HEADS)], axis=0)


def sb_mem_decode_pallas(q_s, q_m, c_sb, c_mem, page_table, cdt=jnp.bfloat16):
    B = q_s.shape[0]
    n_pages = page_table.shape[1]
    ws = 2 * SB_HEADS * HEAD_DIM
    wm = 2 * MEM_HEADS * MEM_HD
    ml = c_mem.shape[1]
    qs = jnp.einsum('bhd,he->bhed', q_s * HEAD_DIM ** -0.5, jnp.eye(SB_HEADS, dtype=q_s.dtype))
    qs = jnp.pad(qs.reshape(B, SB_HEADS, ws // 2), ((0, 0), (0, 0), (0, ws // 2))).astype(cdt)
    qm = jnp.einsum('bhd,he->bhed', q_m, jnp.eye(MEM_HEADS, dtype=q_m.dtype)).reshape(B, MEM_HEADS, wm // 2)
    qm = jnp.pad(qm, ((0, 0), (0, 8 - MEM_HEADS), (0, wm // 2))).astype(cdt)
    per_b = lambda shape: pl.BlockSpec((1,) + shape, lambda b, pt: (b,) + (0,) * len(shape))
    ck = SB_CHUNK_PAGES * PAGE_SIZE
    grid_spec = pltpu.PrefetchScalarGridSpec(
        num_scalar_prefetch=1,
        grid=(B,),
        in_specs=[per_b((SB_HEADS, ws)), per_b((8, wm)), per_b((ml, wm)), pl.BlockSpec(memory_space=pl.ANY)],
        out_specs=[per_b((SB_HEADS, HEAD_DIM)), per_b((MEM_HEADS, MEM_HD))],
        scratch_shapes=[pltpu.VMEM((2, ws, ck), jnp.float32), pltpu.VMEM((ws, ck), jnp.float32),
                        pltpu.SemaphoreType.DMA((3,))],
    )
    o_s, o_m = pl.pallas_call(
        functools.partial(_sb_mem_decode_kernel, n_pages=n_pages, cdt=cdt),
        grid_spec=grid_spec,
        out_shape=[jax.ShapeDtypeStruct((B, SB_HEADS, HEAD_DIM), jnp.float32),
                   jax.ShapeDtypeStruct((B, MEM_HEADS, MEM_HD), jnp.float32)],
        compiler_params=pltpu.CompilerParams(dimension_semantics=("arbitrary",), vmem_limit_bytes=VMEM_LIMIT),
        name="sb_mem_decode",
    )(page_table, qs, qm, c_mem.reshape(B, ml, wm), rows_to_columns(c_sb.reshape(c_sb.shape[0], PAGE_SIZE, ws)))
    return o_s.reshape(B, SB_HEADS * HEAD_DIM), o_m.reshape(B, MEM_HEADS * MEM_HD)


def rel_bucket(dist):
    n = jnp.maximum(dist, 0)
    max_exact = REL_BUCKETS // 2
    nf = jnp.maximum(n, 1).astype(jnp.float32)
    large = max_exact + (jnp.log(nf / max_exact) / math.log(REL_MAX_DIST / max_exact)
                         * (REL_BUCKETS - max_exact)).astype(jnp.int32)
    return jnp.where(n < max_exact, n, jnp.minimum(large, REL_BUCKETS - 1))


def prompt_layer(x, mem, rel_bias, p):
    B, T = x.shape[:2]
    q_n, kv_n, g_n, qkv_s, q_m, g_m = in_proj(x, p['g_attn'], p['w_in_segs'])
    n_cmp = T // CMP_STRIDE - (CMP_LEN // CMP_STRIDE - 1)
    kvc = compress_prompt_pallas(kv_n[:, :, 0:2], p['cmp_wbd'], p['cmp_w2bd'], p['cmp_pe_term'])[:, :n_cmp]
    att_scale = HEAD_DIM ** -0.5
    o_n = nsa_prompt_pallas(q_n * att_scale, g_n, kvc[:, :, 0], kvc[:, :, 1], kv_n[:, :, 2], kv_n[:, :, 3],
                            kv_n[:, :, 4], kv_n[:, :, 5], rel_bias)
    o_s = sb_prompt_pallas(qkv_s * jnp.array([att_scale, 1.0, 1.0], jnp.float32).reshape(1, 1, 3, 1, 1))
    ml = mem.shape[1]
    mem_rows = B * ml
    tm_mem = 256 if mem_rows % 256 == 0 else mem_rows
    mem_kv = rms_proj(mem.reshape(mem_rows, D_MODEL), p['g_mem'], [p['w_mem_kv'].astype(jnp.bfloat16)],
                      sigmoid=(False,), out_dtypes=(jnp.float32,), tm=tm_mem)[0].reshape(B, ml, -1)
    post = post_attn_pallas(x, o_n, o_s, q_m.reshape(B, T, -1), g_m.reshape(B, T, -1), mem_kv, p, with_mem=True)
    wb = min(WINDOW, T)
    return post, (kv_n[:, :, 0:2], kv_n[:, :, 2:4], qkv_s[:, :, 1:3], kv_n[:, T - wb:, 4:6],
                  mem_kv.reshape(B, ml, 2, MEM_HEADS, MEM_HD))


def sample_layer(x, c_cmp, c_slc, c_sb, c_win, c_mem, page_table, rel_bias, p):
    B, S = x.shape[:2]
    assert S == 1, "the decode kernels handle one new token per sequence"
    q_n, kv_n, g_n, qkv_s, q_m, g_m = in_proj(x, p['g_attn'], p['w_in_segs'])
    win_buf = jnp.concatenate([c_win, kv_n[:, :, 4:6]], axis=1)
    n_phys = c_cmp.shape[0]
    hpp = PAGE_SIZE // CMP_STRIDE
    pages_per_step = next(t for t in (64, 32, 16, 8, 4, 2, 1) if n_phys % t == 0)
    h_all = halves_matmul(c_cmp.reshape(n_phys, PAGE_SIZE, -1), p['cmp_wbd'],
                          pages_per_step).reshape(n_phys, hpp, -1)
    o_n = nsa_decode_pallas(q_n[:, 0], g_n[:, 0], kv_n[:, 0], c_win, h_all, c_slc, page_table, rel_bias,
                            p['cmp_w2bd'], p['cmp_pe_term'])
    o_s, o_m = sb_mem_decode_pallas(qkv_s[:, 0, 0], q_m[:, 0], c_sb, c_mem, page_table)
    post = post_attn_pallas(x.reshape(1, B, D_MODEL), o_n[None], o_s[None], o_m[None], g_m.reshape(1, B, -1),
                            None, p, with_mem=False)
    post = [a.reshape(B, S, -1) for a in post]
    return post, (kv_n[:, :, 0:2], kv_n[:, :, 2:4], qkv_s[:, :, 1:3], win_buf[:, S:])


def kernel(x_prompt, x_sample, mem_prompt, cache_nsa_cmp, cache_nsa_slc, cache_sb, cache_nsa_win,
           cache_mem, page_table, rel_bias, g_attn, w_in, cmp_pe_k, cmp_w1_k, cmp_w2_k, cmp_pe_v,
           cmp_w1_v, cmp_w2_v, g_mem, w_mem_kv, w_br_nsa, w_br_sb, w_br_mem, w_out, g_ffn, w_router,
           b_router, w_up, b_up, w_down, b_down, g_final):
    depth = g_attn.shape[0]
    xp, xs = x_prompt, x_sample
    st_p, st_s = [], []
    for l in range(depth):
        p = {'g_attn': g_attn[l], 'w_in_segs': split_w_in(w_in[l]),
             'cmp_pe_k': cmp_pe_k[l], 'cmp_w1_k': cmp_w1_k[l], 'cmp_w2_k': cmp_w2_k[l],
             'cmp_pe_v': cmp_pe_v[l], 'cmp_w1_v': cmp_w1_v[l], 'cmp_w2_v': cmp_w2_v[l],
             'g_mem': g_mem[l], 'w_mem_kv': w_mem_kv[l],
             'w_br_nsa': w_br_nsa[l], 'w_br_sb': w_br_sb[l], 'w_br_mem': w_br_mem[l], 'w_out': w_out[l],
             'g_ffn': g_ffn[l], 'w_router': w_router[l], 'b_router': b_router[l],
             'w_up': w_up[l], 'b_up': b_up[l], 'w_down': w_down[l], 'b_down': b_down[l]}
        p['cmp_wbd'], p['cmp_w2bd'], p['cmp_pe_term'] = compress_weights(p, jnp.bfloat16)
        xp, sp = prompt_layer(xp, mem_prompt, rel_bias, p)
        xs, ss = sample_layer(xs, cache_nsa_cmp[l], cache_nsa_slc[l], cache_sb[l], cache_nsa_win[l],
                              cache_mem[l], page_table, rel_bias, p)
        xp = ffn_block(*xp, p, g_final, last=(l == depth - 1))
        xs = ffn_block(*xs, p, g_final, last=(l == depth - 1))
        st_p.append(sp)
        st_s.append(ss)
    y_prompt, y_sample = xp, xs

    def stacked(states, i):
        return jnp.stack([s[i] for s in states])

    return (y_prompt, y_sample,
            stacked(st_p, 0), stacked(st_p, 1), stacked(st_p, 2), stacked(st_p, 3), stacked(st_p, 4),
            stacked(st_s, 0), stacked(st_s, 1), stacked(st_s, 2), stacked(st_s, 3))
```

```python
import functools
import math

import jax
import jax.numpy as jnp
import numpy as np
from jax import lax
from jax.experimental import pallas as pl
from jax.experimental.pallas import tpu as pltpu

D_MODEL = 1024
PAGE_SIZE = 128
NSA_HEADS = 8
NSA_KV = 2
NSA_HPG = NSA_HEADS // NSA_KV
HEAD_DIM = 64
CMP_LEN = 32
CMP_STRIDE = 16
CMP_HID = 128
SLC_LEN = 64
SLC_TOP = 16
WINDOW = 512
NSA_QBLK = 32
FORCE_SCORE = 1.0e6
SB_HEADS = 8
SB_QBLK = 128
MEM_HEADS = 4
MEM_HD = 128
REL_BUCKETS = 32
REL_MAX_DIST = 2048
N_EXPERTS = 32
TOP_K = 4
D_FF = 1024
SWIGLU_LIMIT = 7.0
SWIGLU_ALPHA = 1.702
MOE_BLOCK = 512
EPS = 1e-6
NEG_INF = -1.0e30

SEG_QN = NSA_HEADS * HEAD_DIM
SEG_KVN = 6 * NSA_KV * HEAD_DIM
SEG_GN = 3 * NSA_HEADS
SEG_QKVS = 3 * SB_HEADS * HEAD_DIM
SEG_QM = MEM_HEADS * MEM_HD
SEG_GM = 3 * D_MODEL
IN_SIZES = (SEG_QN, SEG_KVN, SEG_GN, SEG_QKVS, SEG_QM, SEG_GM)

VMEM_LIMIT = 48 * 1024 * 1024


def _rms_proj_kernel(x_ref, g_ref, *refs, n_seg, sigmoid):
    w_refs = refs[:n_seg]
    o_refs = refs[n_seg:]
    x = x_ref[...]
    h = x * lax.rsqrt(jnp.mean(x * x, axis=-1, keepdims=True) + EPS) * g_ref[...]
    hb = h.astype(jnp.bfloat16)
    for w_ref, o_ref, sig in zip(w_refs, o_refs, sigmoid):
        z = jnp.dot(hb, w_ref[...], preferred_element_type=jnp.float32)
        if sig:
            z = jax.nn.sigmoid(z)
        o_ref[...] = z.astype(o_ref.dtype)


def rms_proj(x2d, g, weights, sigmoid, out_dtypes, tm):
    n, d = x2d.shape
    assert n % tm == 0
    n_seg = len(weights)
    in_specs = [pl.BlockSpec((tm, d), lambda i: (i, 0)),
                pl.BlockSpec((1, d), lambda i: (0, 0))]
    for w in weights:
        in_specs.append(pl.BlockSpec(w.shape, lambda i: (0, 0), pipeline_mode=pl.Buffered(1)))
    out_specs = [pl.BlockSpec((tm, w.shape[1]), lambda i: (i, 0)) for w in weights]
    out_shape = [jax.ShapeDtypeStruct((n, w.shape[1]), dt) for w, dt in zip(weights, out_dtypes)]
    return pl.pallas_call(
        functools.partial(_rms_proj_kernel, n_seg=n_seg, sigmoid=tuple(sigmoid)),
        grid=(n // tm,),
        in_specs=in_specs,
        out_specs=out_specs,
        out_shape=out_shape,
        compiler_params=pltpu.CompilerParams(dimension_semantics=("parallel",),
                                             vmem_limit_bytes=VMEM_LIMIT),
        name="rms_proj",
    )(x2d, g.reshape(1, d), *weights)


def in_proj(x, g_attn, w_segs):
    B, T, D = x.shape
    n = B * T
    tm = 256 if n % 256 == 0 else n
    outs = rms_proj(x.reshape(n, D), g_attn, w_segs,
                    sigmoid=(False, False, True, False, False, True),
                    out_dtypes=(jnp.float32,) * 6, tm=tm)
    q_n, kv_n, g_n, qkv_s, q_m, g_m = outs
    return (q_n.reshape(B, T, NSA_KV, NSA_HPG, HEAD_DIM),
            kv_n.reshape(B, T, 6, NSA_KV, HEAD_DIM),
            g_n.reshape(B, T, NSA_KV, NSA_HPG, 3),
            qkv_s.reshape(B, T, 3, SB_HEADS, HEAD_DIM),
            q_m.reshape(B, T, MEM_HEADS, MEM_HD),
            g_m.reshape(B, T, 3, D_MODEL))


def split_w_in(w_in):
    offs = np.cumsum((0,) + IN_SIZES)
    return [w_in[:, int(a):int(b)].astype(jnp.bfloat16) for a, b in zip(offs[:-1], offs[1:])]


def _nt_dot(a, b):
    return lax.dot_general(a, b, (((1,), (1,)), ((), ())), preferred_element_type=jnp.float32)


def _online_step(s, mask, v_aug, m, acc, cdt):
    h, tq, tk = s.shape
    s = jnp.where(mask[None], s, NEG_INF)
    m_new = jnp.maximum(m, jnp.max(s, axis=-1, keepdims=True))
    p = jnp.where(mask[None], jnp.exp(s - m_new), 0.0)
    alpha = jnp.exp(m - m_new).reshape(h * tq, 1)
    pv = jnp.dot(p.reshape(h * tq, tk).astype(cdt), v_aug, preferred_element_type=jnp.float32)
    return m_new, alpha * acc + pv


def _finish(acc):
    l = acc[:, HEAD_DIM:HEAD_DIM + 1]
    return acc[:, :HEAD_DIM] / jnp.where(l > 0.0, l, 1.0)


def _nsa_prompt_kernel(q_ref, gate_ref, kc_ref, vc_ref, ks_ref, vs_ref, kw_ref, vw_ref, cb_ref, tb_ref,
                       o_ref, *, tq, n_cmp, ns, cdt):
    f32 = jnp.float32
    qi = pl.program_id(2)
    q0 = qi * tq
    tk = tq
    hpg = NSA_HPG
    rows = hpg * tq
    q = q_ref[0].reshape(rows, HEAD_DIM)

    kc = kc_ref[0, 0]
    ncp = kc.shape[0]
    qpos_c = q0 + lax.broadcasted_iota(jnp.int32, (tq, ncp), 0)
    cidx = lax.broadcasted_iota(jnp.int32, (tq, ncp), 1)
    cmask = (cidx * CMP_STRIDE + (CMP_LEN - 1) <= qpos_c) & (cidx < n_cmp)
    s_c = (_nt_dot(q, kc) + cb_ref[0, 0]).reshape(hpg, tq, ncp)
    s_c = jnp.where(cmask[None], s_c, NEG_INF)
    m_c = jnp.max(s_c, axis=-1, keepdims=True)
    p_c = jnp.where(cmask[None], jnp.exp(s_c - m_c), 0.0)
    l_c = jnp.sum(p_c, axis=-1, keepdims=True)
    p_c = p_c / jnp.where(l_c > 0.0, l_c, 1.0)
    o_c = jnp.dot(p_c.reshape(rows, ncp).astype(cdt), vc_ref[0, 0], preferred_element_type=f32)

    p_sum = p_c[0]
    for h in range(1, hpg):
        p_sum = p_sum + p_c[h]
    jb = lax.broadcasted_iota(jnp.int32, (ns, ncp), 0)
    cc = lax.broadcasted_iota(jnp.int32, (ns, ncp), 1)
    ov_t = ((cc * CMP_STRIDE < jb * SLC_LEN + SLC_LEN)
            & (cc * CMP_STRIDE + (CMP_LEN - 1) >= jb * SLC_LEN)).astype(cdt)
    p_hi = p_sum.astype(cdt)
    p_lo = (p_sum - p_hi.astype(f32)).astype(cdt)
    imp = _nt_dot(ov_t, p_hi) + _nt_dot(ov_t, p_lo)
    blk = lax.broadcasted_iota(jnp.int32, (ns, tq), 0)
    qp_t = q0 + lax.broadcasted_iota(jnp.int32, (ns, tq), 1)
    cur = qp_t // SLC_LEN
    forced = (blk == 0) | (blk == cur) | (blk == cur - 1)
    imp = jnp.where(forced, FORCE_SCORE, imp)
    imp = jnp.where(blk * SLC_LEN <= qp_t, imp, -1.0)
    cnt = jnp.zeros((ns, tq), f32)
    for i in range(ns):
        row = imp[i:i + 1, :]
        ahead = (row > imp) | ((row == imp) & (blk > i))
        cnt = cnt + jnp.where(ahead, 1.0, 0.0)
    sel_t = jnp.where((cnt < float(min(SLC_TOP, ns))) & (imp >= 0.0), 1.0, 0.0)
    sel = sel_t.T.astype(cdt)

    m0 = jnp.full((hpg, tq, 1), NEG_INF, f32)
    a0 = jnp.zeros((rows, 2 * HEAD_DIM), f32)

    tk2 = 2 * tk
    qpos2 = q0 + lax.broadcasted_iota(jnp.int32, (tq, tk2), 0)
    kcol2 = lax.broadcasted_iota(jnp.int32, (tq, tk2), 1)
    eb2 = lax.broadcasted_iota(jnp.int32, (ns, tk2), 0)
    ek2 = lax.broadcasted_iota(jnp.int32, (ns, tk2), 1)

    def sel_body(kp, carry):
        m, acc = carry
        k0 = pl.multiple_of(kp * tk2, tk2)
        d0 = qi - 2 * kp
        bias = jnp.concatenate([tb_ref[0, d0], tb_ref[0, jnp.maximum(d0 - 1, 0)]], axis=-1)
        s = (_nt_dot(q, ks_ref[0, 0, pl.ds(k0, tk2), :]) + bias).reshape(hpg, tq, tk2)
        expand = jnp.where(eb2 == (k0 + ek2) // SLC_LEN, 1.0, 0.0).astype(cdt)
        picked = jnp.dot(sel, expand, preferred_element_type=f32)
        mask = (picked > 0.5) & (k0 + kcol2 <= qpos2)
        return _online_step(s, mask, vs_ref[0, 0, pl.ds(k0, tk2), :], m, acc, cdt)

    _, acc_s = lax.fori_loop(0, qi // 2 + 1, sel_body, (m0, a0))
    o_s = _finish(acc_s)

    first_w = jnp.maximum(qi - (WINDOW // tk + 1), 0)

    def win_body(j, carry):
        m, acc = carry
        tile = first_w + 2 * j
        k0 = pl.multiple_of(tile * tk, tk)
        d0 = qi - tile
        bias = jnp.concatenate([tb_ref[0, d0], tb_ref[0, jnp.maximum(d0 - 1, 0)]], axis=-1)
        s = (_nt_dot(q, kw_ref[0, 0, pl.ds(k0, tk2), :]) + bias).reshape(hpg, tq, tk2)
        dist = qpos2 - (k0 + kcol2)
        mask = (dist >= 0) & (dist < WINDOW)
        return _online_step(s, mask, vw_ref[0, 0, pl.ds(k0, tk2), :], m, acc, cdt)

    _, acc_w = lax.fori_loop(0, (qi - first_w) // 2 + 1, win_body, (m0, a0))
    o_w = _finish(acc_w)

    gates = gate_ref[0, 0]
    outs = []
    for h in range(hpg):
        r = slice(h * tq, (h + 1) * tq)
        outs.append(gates[:, 3 * h:3 * h + 1] * o_c[r]
                    + gates[:, 3 * h + 1:3 * h + 2] * o_s[r]
                    + gates[:, 3 * h + 2:3 * h + 3] * o_w[r])
    o_ref[0] = jnp.concatenate(outs, axis=-1).astype(o_ref.dtype)


def _aug_ones(v):
    return jnp.concatenate([v, jnp.ones_like(v)], axis=-1)


def nsa_bias_tables(rel_bias, t_len, tq):
    nq = t_len // tq
    ncp = t_len // CMP_STRIDE
    G, H = NSA_KV, NSA_HPG
    bias_d = rel_bias[rel_bucket(jnp.arange(t_len, dtype=jnp.int32))]
    bias_d = bias_d.reshape(t_len, G, H).transpose(1, 2, 0)
    q0 = jnp.arange(nq)[:, None] * tq

    def skew(rows, n):
        length = rows.shape[-1]
        lead = rows.shape[:-1]
        flat = jnp.broadcast_to(rows[..., None, :], lead + (n, length)).reshape(lead + (n * length,))
        return flat[..., :n * (length - 1)].reshape(lead + (n, length - 1))

    l = jnp.arange(2 * tq + 1)[None, :]
    w = bias_d[:, :, jnp.clip(q0 + (tq - 1) - l, 0, t_len - 1)]
    tb = skew(w.transpose(0, 2, 1, 3), tq)[..., tq - 1:2 * tq - 1]
    tb = tb.reshape(G, nq, H * tq, tq)
    A = tq // CMP_STRIDE
    b = jnp.arange(CMP_STRIDE)[None, :, None]
    mm = jnp.arange(ncp + A)[None, None, :]
    d_c = q0[:, :, None] + b - ((mm - (A - 1)) * CMP_STRIDE + (CMP_LEN - 1))
    f = bias_d[:, :, jnp.clip(d_c, 0, t_len - 1)]
    cb = skew(f.transpose(0, 2, 1, 3, 4), A)[..., A - 1:A - 1 + ncp]
    cb = cb.transpose(0, 1, 2, 4, 3, 5).reshape(G, nq, H * tq, ncp)
    return tb, cb


def nsa_prompt_pallas(q, gates, kc, vc, k_slc, v_slc, k_win, v_win, rel_bias, *, tq=128, cdt=jnp.bfloat16):
    B, T = q.shape[:2]
    n_cmp = kc.shape[1]
    ncp = T // CMP_STRIDE
    ns = T // SLC_LEN
    nq = T // tq
    assert nq % 2 == 0, "the selected-branch sweep reads key tiles in pairs"
    rows = NSA_HPG * tq
    tb, cb = nsa_bias_tables(rel_bias, T, tq)
    to_g = lambda a: a.astype(cdt).transpose(0, 2, 1, 3)
    qh = q.astype(cdt).reshape(B, T, NSA_HEADS, HEAD_DIM).transpose(0, 2, 1, 3)
    gt = gates.astype(jnp.float32).reshape(B, T, NSA_KV, NSA_HPG * 3).transpose(0, 2, 1, 3)
    pad_c = ((0, 0), (0, ncp - n_cmp), (0, 0), (0, 0))
    kcg = to_g(jnp.pad(kc, pad_c))
    vcg = to_g(jnp.pad(vc, pad_c))
    ksg, kwg = to_g(k_slc), to_g(k_win)
    vsg, vwg = _aug_ones(to_g(v_slc)), _aug_ones(to_g(v_win))
    per_bg = lambda w: pl.BlockSpec((1, 1, T, w), lambda g, b, i: (b, g, 0, 0))
    per_bg_c = lambda w: pl.BlockSpec((1, 1, ncp, w), lambda g, b, i: (b, g, 0, 0))
    return pl.pallas_call(
        functools.partial(_nsa_prompt_kernel, tq=tq, n_cmp=n_cmp, ns=ns, cdt=cdt),
        grid=(NSA_KV, B, nq),
        in_specs=[
            pl.BlockSpec((1, NSA_HPG, tq, HEAD_DIM), lambda g, b, i: (b, g, i, 0)),
            pl.BlockSpec((1, 1, tq, NSA_HPG * 3), lambda g, b, i: (b, g, i, 0)),
            per_bg_c(HEAD_DIM), per_bg_c(HEAD_DIM),
            per_bg(HEAD_DIM), per_bg(2 * HEAD_DIM), per_bg(HEAD_DIM), per_bg(2 * HEAD_DIM),
            pl.BlockSpec((1, 1, rows, ncp), lambda g, b, i: (g, i, 0, 0)),
            pl.BlockSpec((1, nq, rows, tq), lambda g, b, i: (g, 0, 0, 0), pipeline_mode=pl.Buffered(1)),
        ],
        out_specs=pl.BlockSpec((1, tq, NSA_HPG * HEAD_DIM), lambda g, b, i: (b, i, g)),
        out_shape=jax.ShapeDtypeStruct((B, T, NSA_HEADS * HEAD_DIM), cdt),
        compiler_params=pltpu.CompilerParams(dimension_semantics=("parallel", "parallel", "parallel"),
                                             vmem_limit_bytes=VMEM_LIMIT),
        name="nsa_prompt",
    )(qh, gt, kcg, vcg, ksg, vsg, kwg, vwg, cb, tb)


SB_DEAD_LOG = -104.0


def _split3(x, cdt):
    hi = x.astype(cdt)
    r = x - hi.astype(jnp.float32)
    mid = r.astype(cdt)
    lo = (r - mid.astype(jnp.float32)).astype(cdt)
    return hi, mid, lo


def _sb_tile(z, causal, carry, v, tri, cdt):
    h, tq, tk = z.shape
    sp = jnp.maximum(z, 0.0) + jnp.log1p(jnp.exp(-jnp.abs(z)))
    l1m = jnp.where(causal[None], -sp, 0.0).reshape(h * tq, tk)
    suffix = None
    for part in _split3(l1m, cdt):
        d = jnp.dot(part, tri, preferred_element_type=jnp.float32)
        suffix = d if suffix is None else suffix + d
    surv = (suffix + carry).reshape(h, tq, tk)
    a = jnp.where(causal[None], jnp.exp(z - sp + surv), 0.0)
    pv = jnp.einsum('hqk,hkd->hqd', a.astype(cdt), v, preferred_element_type=jnp.float32)
    return pv, carry + suffix[:, 0:1] + l1m[:, 0:1]


def _sb_prompt_kernel(q_ref, k_ref, v_ref, o_ref, *, tq, hs, cdt):
    f32 = jnp.float32
    qi = pl.program_id(2)
    q0 = qi * tq
    tk = tq
    q = q_ref[0, 0]
    qpos = q0 + lax.broadcasted_iota(jnp.int32, (tq, tk), 0)
    kcol = lax.broadcasted_iota(jnp.int32, (tq, tk), 1)
    tri = jnp.where(lax.broadcasted_iota(jnp.int32, (tk, tk), 0) > lax.broadcasted_iota(jnp.int32, (tk, tk), 1),
                    1.0, 0.0).astype(cdt)

    def cond(state):
        kb, live, _, _ = state
        return (kb >= 0) & (live > SB_DEAD_LOG)

    def body(state):
        kb, _, carry, acc = state
        k0 = pl.multiple_of(kb * tk, tk)
        k = k_ref[0, 0, :, pl.ds(k0, tk), :]
        v = v_ref[0, 0, :, pl.ds(k0, tk), :]
        z = jnp.einsum('hqd,hkd->hqk', q, k, preferred_element_type=f32)
        pv, carry = _sb_tile(z, k0 + kcol < qpos, carry, v, tri, cdt)
        return kb - 1, jnp.max(carry), carry, acc + pv

    init = (qi, jnp.float32(0.0), jnp.zeros((hs * tq, 1), f32), jnp.zeros((hs, tq, HEAD_DIM), f32))
    _, _, _, acc = lax.while_loop(cond, body, init)
    o_ref[0] = jnp.concatenate([acc[h] for h in range(hs)], axis=-1).astype(o_ref.dtype)


def sb_prompt_pallas(qkv, *, tq=128, hs=4, cdt=jnp.bfloat16):
    B, T = qkv.shape[:2]
    x = qkv.astype(cdt).transpose(0, 2, 3, 1, 4)
    full = lambda j: pl.BlockSpec((1, 1, hs, T, HEAD_DIM), lambda b, g, i: (b, j, g, 0, 0))
    return pl.pallas_call(
        functools.partial(_sb_prompt_kernel, tq=tq, hs=hs, cdt=cdt),
        grid=(B, SB_HEADS // hs, T // tq),
        in_specs=[pl.BlockSpec((1, 1, hs, tq, HEAD_DIM), lambda b, g, i: (b, 0, g, i, 0)), full(1), full(2)],
        out_specs=pl.BlockSpec((1, tq, hs * HEAD_DIM), lambda b, g, i: (b, i, g)),
        out_shape=jax.ShapeDtypeStruct((B, T, SB_HEADS * HEAD_DIM), cdt),
        compiler_params=pltpu.CompilerParams(dimension_semantics=("parallel", "parallel", "parallel"),
                                             vmem_limit_bytes=VMEM_LIMIT),
        name="sb_prompt",
    )(x, x, x)


def _moe_kernel(blk_e_ref, n_used_ref, tok_ref, dst_ref, rw_ref, h_hbm, wup_ref, bup_ref, wdn_ref, bdn_ref,
                parts_hbm, xbuf, obuf, sem_in, sem_out, *, blk, cdt):
    j = pl.program_id(0)

    def gather_copy(r):
        return pltpu.make_async_copy(h_hbm.at[pl.ds(tok_ref[0, 0, r], 1)], xbuf.at[pl.ds(r, 1)], sem_in)

    def scatter_copy(r):
        return pltpu.make_async_copy(obuf.at[pl.ds(r, 1)], parts_hbm.at[pl.ds(dst_ref[0, 0, r], 1)], sem_out)

    def each_row(fn):
        def body(r, c):
            fn(r)
            return c
        lax.fori_loop(0, blk, body, 0, unroll=8)

    @pl.when(j < n_used_ref[0])
    def _():
        each_row(lambda r: gather_copy(r).start())
        each_row(lambda r: gather_copy(r).wait())
        x = xbuf[...].astype(cdt)
        gu = jnp.dot(x, wup_ref[0], preferred_element_type=jnp.float32) + bup_ref[0]
        gate = jnp.minimum(gu[:, :D_FF], SWIGLU_LIMIT)
        up = jnp.clip(gu[:, D_FF:], -SWIGLU_LIMIT, SWIGLU_LIMIT)
        act = (up + 1.0) * gate * jax.nn.sigmoid(SWIGLU_ALPHA * gate)
        out = jnp.dot(act.astype(cdt), wdn_ref[0], preferred_element_type=jnp.float32) + bdn_ref[0]
        obuf[...] = out * rw_ref[...]
        each_row(lambda r: scatter_copy(r).start())
        each_row(lambda r: scatter_copy(r).wait())


def moe_ffn_pallas(h_pad, blk_e, n_used, row_tok, row_dst, row_w, w_up, b_up, w_down, b_down, *, blk, n_assign,
                   cdt=jnp.bfloat16):
    n_rows = row_tok.shape[0]
    n_blocks = n_rows // blk
    d = h_pad.shape[1]
    smem_rows = lambda: pl.BlockSpec((1, 1, blk), lambda j, be, nu: (j, 0, 0), memory_space=pltpu.SMEM)
    grid_spec = pltpu.PrefetchScalarGridSpec(
        num_scalar_prefetch=2,
        grid=(n_blocks,),
        in_specs=[
            smem_rows(), smem_rows(),
            pl.BlockSpec((blk, 1), lambda j, be, nu: (j, 0)),
            pl.BlockSpec(memory_space=pl.ANY),
            pl.BlockSpec((1, d, 2 * D_FF), lambda j, be, nu: (be[j], 0, 0)),
            pl.BlockSpec((1, 1, 2 * D_FF), lambda j, be, nu: (be[j], 0, 0)),
            pl.BlockSpec((1, D_FF, d), lambda j, be, nu: (be[j], 0, 0)),
            pl.BlockSpec((1, 1, d), lambda j, be, nu: (be[j], 0, 0)),
        ],
        out_specs=pl.BlockSpec(memory_space=pl.ANY),
        scratch_shapes=[pltpu.VMEM((blk, d), jnp.float32), pltpu.VMEM((blk, d), jnp.float32),
                        pltpu.SemaphoreType.DMA(()), pltpu.SemaphoreType.DMA(())],
    )
    return pl.pallas_call(
        functools.partial(_moe_kernel, blk=blk, cdt=cdt),
        grid_spec=grid_spec,
        out_shape=jax.ShapeDtypeStruct((n_assign + blk, d), jnp.float32),
        compiler_params=pltpu.CompilerParams(dimension_semantics=("arbitrary",), vmem_limit_bytes=VMEM_LIMIT),
        name="moe_ffn",
    )(blk_e, n_used, row_tok.reshape(n_blocks, 1, blk), row_dst.reshape(n_blocks, 1, blk),
      row_w.reshape(n_rows, 1), h_pad, w_up.astype(cdt), b_up.reshape(N_EXPERTS, 1, 2 * D_FF),
      w_down.astype(cdt), b_down.reshape(N_EXPERTS, 1, d))


def _post_attn_kernel(x_ref, on_ref, os_ref, m_ref, gm_ref, memkv_ref, wn_ref, ws_ref, wm_ref, wo_ref, gf_ref,
                      wr_ref, br_ref, x1_ref, hf_ref, lg_ref, *, with_mem, cdt):
    f32 = jnp.float32
    if with_mem:
        qm = m_ref[0].astype(cdt)
        mem = memkv_ref[0]
        half = MEM_HEADS * MEM_HD
        heads = []
        for h in range(MEM_HEADS):
            c = slice(h * MEM_HD, (h + 1) * MEM_HD)
            s = _nt_dot(qm[:, c], mem[:, c]) * (MEM_HD ** -0.5)
            pm = jnp.exp(s - jnp.max(s, axis=-1, keepdims=True))
            pm = pm / jnp.sum(pm, axis=-1, keepdims=True)
            heads.append(jnp.dot(pm.astype(cdt), mem[:, half + h * MEM_HD:half + (h + 1) * MEM_HD],
                                 preferred_element_type=f32))
        o_m = jnp.concatenate(heads, axis=-1)
    else:
        o_m = m_ref[0]
    d = x_ref.shape[-1]
    gm = gm_ref[0]
    y = (gm[:, :d] * jnp.dot(on_ref[0].astype(cdt), wn_ref[...], preferred_element_type=f32)
         + gm[:, d:2 * d] * jnp.dot(os_ref[0].astype(cdt), ws_ref[...], preferred_element_type=f32)
         + gm[:, 2 * d:] * jnp.dot(o_m.astype(cdt), wm_ref[...], preferred_element_type=f32))
    x1 = x_ref[0] + jnp.dot(y.astype(cdt), wo_ref[...], preferred_element_type=f32)
    x1_ref[0] = x1
    hf = x1 * lax.rsqrt(jnp.mean(x1 * x1, axis=-1, keepdims=True) + EPS) * gf_ref[...]
    hf_ref[0] = hf
    lg_ref[0] = jnp.dot(hf.astype(cdt), wr_ref[...], preferred_element_type=f32) + br_ref[...]


def post_attn_pallas(x, o_n, o_s, m, g_m, mem_kv, p, *, with_mem, cdt=jnp.bfloat16):
    B, T, d = x.shape
    tm = 256 if T % 256 == 0 else T
    if mem_kv is None:
        mem_kv = jnp.zeros((B, 8, 2 * MEM_HEADS * MEM_HD), cdt)
    ml = mem_kv.shape[1]
    rows = lambda w: pl.BlockSpec((1, tm, w), lambda b, i: (b, i, 0))
    const = lambda a: pl.BlockSpec(a.shape, lambda b, i: (0,) * a.ndim)
    weights = [p['w_br_nsa'].astype(cdt), p['w_br_sb'].astype(cdt), p['w_br_mem'].astype(cdt),
               p['w_out'].astype(cdt), p['g_ffn'].reshape(1, d), p['w_router'].astype(cdt),
               p['b_router'].reshape(1, N_EXPERTS)]
    return pl.pallas_call(
        functools.partial(_post_attn_kernel, with_mem=with_mem, cdt=cdt),
        grid=(B, T // tm),
        in_specs=[rows(d), rows(o_n.shape[-1]), rows(o_s.shape[-1]), rows(m.shape[-1]), rows(3 * d),
                  pl.BlockSpec((1, ml, mem_kv.shape[-1]), lambda b, i: (b, 0, 0))] + [const(w) for w in weights],
        out_specs=[rows(d), rows(d), rows(N_EXPERTS)],
        out_shape=[jax.ShapeDtypeStruct((B, T, d), jnp.float32), jax.ShapeDtypeStruct((B, T, d), jnp.float32),
                   jax.ShapeDtypeStruct((B, T, N_EXPERTS), jnp.float32)],
        compiler_params=pltpu.CompilerParams(dimension_semantics=("parallel", "parallel"),
                                             vmem_limit_bytes=VMEM_LIMIT),
        name="post_attn",
    )(x, o_n, o_s, m, g_m, mem_kv.astype(cdt), *weights)


def _combine_norm_kernel(x_ref, *refs, normalize):
    part_refs, g_ref, o_ref = refs[:TOP_K], refs[TOP_K], refs[TOP_K + 1]
    x = x_ref[...]
    for part_ref in part_refs:
        x = x + part_ref[...]
    if normalize:
        x = x * lax.rsqrt(jnp.mean(x * x, axis=-1, keepdims=True) + EPS) * g_ref[...]
    o_ref[...] = x


def combine_norm_pallas(x, parts, g_final, tm, normalize):
    n, d = x.shape
    nt = n // tm
    part_spec = lambda k: pl.BlockSpec((tm, d), lambda i: (k * nt + i, 0))
    return pl.pallas_call(
        functools.partial(_combine_norm_kernel, normalize=normalize),
        grid=(nt,),
        in_specs=[pl.BlockSpec((tm, d), lambda i: (i, 0))] + [part_spec(k) for k in range(TOP_K)]
                 + [pl.BlockSpec((1, d), lambda i: (0, 0))],
        out_specs=pl.BlockSpec((tm, d), lambda i: (i, 0)),
        out_shape=jax.ShapeDtypeStruct((n, d), jnp.float32),
        compiler_params=pltpu.CompilerParams(dimension_semantics=("parallel",), vmem_limit_bytes=VMEM_LIMIT),
        name="combine_norm",
    )(x, *([parts] * TOP_K), g_final.reshape(1, d))


def ffn_block(x, hf, logits, p, g_final, last):
    lead = x.shape[:-1]
    x2 = x.reshape(-1, D_MODEL)
    hf = hf.reshape(-1, D_MODEL)
    logits = logits.reshape(-1, N_EXPERTS)
    N = x2.shape[0]
    top_l, top_e = lax.top_k(logits, TOP_K)
    top_w = jax.nn.softmax(top_l, axis=-1)
    n_assign = N * TOP_K
    blk = int(min(MOE_BLOCK, max(8, n_assign // N_EXPERTS)))
    n_rows = (n_assign // blk + N_EXPERTS) * blk
    n_blocks = n_rows // blk
    flat_e = top_e.reshape(-1)
    order = jnp.argsort(flat_e).astype(jnp.int32)
    se = flat_e[order]
    counts = jnp.bincount(flat_e, length=N_EXPERTS)
    start = jnp.cumsum(counts) - counts
    padded = (counts + blk - 1) // blk * blk
    p_end = jnp.cumsum(padded)
    p_start = p_end - padded
    dest = p_start[se] + jnp.arange(n_assign) - start[se]
    aid = jnp.full((n_rows,), n_assign, jnp.int32).at[dest].set(order)
    real = aid < n_assign
    row_tok = jnp.where(real, aid // TOP_K, N)
    row_dst = jnp.where(real, (aid % TOP_K) * N + aid // TOP_K, n_assign + jnp.arange(n_rows) % blk)
    row_w = jnp.where(real, top_w.reshape(-1)[jnp.minimum(aid, n_assign - 1)], 0.0)
    blk_e = jnp.minimum(jnp.searchsorted(p_end, jnp.arange(n_blocks) * blk, side='right'),
                        N_EXPERTS - 1).astype(jnp.int32)
    n_used = (p_end[-1] // blk).astype(jnp.int32).reshape(1)
    h_pad = jnp.concatenate([hf, jnp.zeros((1, D_MODEL), hf.dtype)], axis=0)
    parts = moe_ffn_pallas(h_pad, blk_e, n_used, row_tok, row_dst, row_w, p['w_up'], p['b_up'],
                           p['w_down'], p['b_down'], blk=blk, n_assign=n_assign)
    tm = 256 if N % 256 == 0 else N
    y = combine_norm_pallas(x2, parts, g_final, tm, normalize=last)
    return y.reshape(*lead, D_MODEL)


def _rows_matmul_kernel(x_ref, w_ref, o_ref):
    o_ref[...] = jnp.dot(x_ref[...].astype(w_ref.dtype), w_ref[...], preferred_element_type=jnp.float32)


def rows_matmul(x, w, tm):
    r, k = x.shape
    n = w.shape[1]
    return pl.pallas_call(
        _rows_matmul_kernel,
        grid=(r // tm,),
        in_specs=[pl.BlockSpec((tm, k), lambda i: (i, 0)),
                  pl.BlockSpec((k, n), lambda i: (0, 0), pipeline_mode=pl.Buffered(1))],
        out_specs=pl.BlockSpec((tm, n), lambda i: (i, 0)),
        out_shape=jax.ShapeDtypeStruct((r, n), jnp.float32),
        compiler_params=pltpu.CompilerParams(dimension_semantics=("parallel",), vmem_limit_bytes=VMEM_LIMIT),
        name="rows_matmul",
    )(x, w)


def _halves_matmul_kernel(*refs):
    x_refs, w_ref, o_ref = refs[:-2], refs[-2], refs[-1]
    n = o_ref.shape[0]
    pages, page = x_refs[0].shape[:2]
    strips = len(x_refs)
    acc = None
    for s in range(CMP_STRIDE):
        for c, x_ref in enumerate(x_refs):
            xs = x_ref[:, pl.ds(s, page // CMP_STRIDE, stride=CMP_STRIDE), :].reshape(n, x_ref.shape[2])
            d = jnp.dot(xs.astype(w_ref.dtype), w_ref[s * strips + c], preferred_element_type=jnp.float32)
            acc = d if acc is None else acc + d
    o_ref[...] = acc


def halves_matmul(x, wbd, pages_per_step):
    n_pg, page, width = x.shape
    n = wbd.shape[1]
    lane = 128
    strips = width // lane
    strip_spec = lambda c: pl.BlockSpec((pages_per_step, page, lane), lambda i: (i, 0, c))
    return pl.pallas_call(
        _halves_matmul_kernel,
        grid=(n_pg // pages_per_step,),
        in_specs=[strip_spec(c) for c in range(strips)]
                 + [pl.BlockSpec((CMP_STRIDE * strips, lane, n), lambda i: (0, 0, 0), pipeline_mode=pl.Buffered(1))],
        out_specs=pl.BlockSpec((pages_per_step * page // CMP_STRIDE, n), lambda i: (i, 0)),
        out_shape=jax.ShapeDtypeStruct((n_pg * page // CMP_STRIDE, n), jnp.float32),
        compiler_params=pltpu.CompilerParams(dimension_semantics=("parallel",), vmem_limit_bytes=VMEM_LIMIT),
        name="halves_matmul",
    )(*([x] * strips), wbd.reshape(CMP_STRIDE * strips, lane, n))


def compress_weights(p, cdt):
    R = CMP_LEN // CMP_STRIDE
    nc4 = 2 * NSA_KV
    w1 = jnp.stack([p['cmp_w1_k'], p['cmp_w1_k'], p['cmp_w1_v'], p['cmp_w1_v']])
    w1 = w1.reshape(nc4, R, CMP_STRIDE, HEAD_DIM, CMP_HID)
    eye = jnp.eye(nc4, dtype=w1.dtype)
    wbd = jnp.einsum('crsdh,ce->scdreh', w1, eye).reshape(CMP_STRIDE * nc4 * HEAD_DIM, R * nc4 * CMP_HID)
    w2 = jnp.stack([p['cmp_w2_k'], p['cmp_w2_k'], p['cmp_w2_v'], p['cmp_w2_v']])
    w2bd = jnp.einsum('chd,ce->ched', w2, eye).reshape(nc4 * CMP_HID, nc4 * HEAD_DIM)
    pe = jnp.stack([p['cmp_pe_k'].reshape(-1) @ p['cmp_w1_k'], p['cmp_pe_k'].reshape(-1) @ p['cmp_w1_k'],
                    p['cmp_pe_v'].reshape(-1) @ p['cmp_w1_v'], p['cmp_pe_v'].reshape(-1) @ p['cmp_w1_v']])
    return wbd.astype(cdt), w2bd.astype(cdt), pe.reshape(1, nc4 * CMP_HID)


def rows_to_columns(pages):
    return pages.transpose(0, 2, 1)


def _softmax_with_extra(s, mask, s_x):
    s = jnp.where(mask, s, NEG_INF)
    m = jnp.maximum(jnp.max(s, axis=-1, keepdims=True), s_x)
    p = jnp.where(mask, jnp.exp(s - m), 0.0)
    p_x = jnp.exp(s_x - m)
    inv = 1.0 / (jnp.sum(p, axis=-1, keepdims=True) + p_x)
    return p * inv, p_x * inv


def _nsa_decode_kernel(pt_ref, qk_ref, gate_ref, news_ref, neww_ref, win_ref, pe_ref, w2_ref, cbias_ref,
                       sbias_ref, wbias_ref, b0_ref, e_ref, h_hbm, slc_hbm, o_ref, hbuf, sbuf, sem,
                       *, n_pages, n_cmp, ns, cdt):
    f32 = jnp.float32
    b = pl.program_id(0)
    nb = pl.num_programs(0)
    slot = b % 2
    hpp = PAGE_SIZE // CMP_STRIDE

    def copies(bb, sl):
        out = []
        for p in range(n_pages):
            pg = pt_ref[bb, p]
            out.append(pltpu.make_async_copy(h_hbm.at[pg], hbuf.at[sl, pl.ds(p * hpp, hpp)], sem.at[0, sl]))
            out.append(pltpu.make_async_copy(slc_hbm.at[pg], sbuf.at[sl, :, pl.ds(p * PAGE_SIZE, PAGE_SIZE)],
                                             sem.at[1, sl]))
        return out

    @pl.when(b == 0)
    def _():
        for c in copies(0, 0):
            c.start()

    @pl.when(b + 1 < nb)
    def _():
        for c in copies(b + 1, 1 - slot):
            c.start()

    for c in copies(b, slot):
        c.wait()

    qk = qk_ref[0]
    nh = qk.shape[0]
    width = qk.shape[1]
    row = lax.broadcasted_iota(jnp.int32, (nh, 1), 0)

    def own_values(o_full):
        v0 = o_full[:, NSA_KV * HEAD_DIM:NSA_KV * HEAD_DIM + HEAD_DIM]
        v1 = o_full[:, NSA_KV * HEAD_DIM + HEAD_DIM:]
        return jnp.where(row < NSA_HPG, v0, v1)

    hh = hbuf[slot]
    ncp = hh.shape[0]
    half = hh.shape[1] // 2
    pre = hh[:, :half] + pltpu.roll(hh[:, half:], ncp - 1, 0) + pe_ref[...]
    kvc = jnp.dot(jax.nn.gelu(pre).astype(cdt), w2_ref[...], preferred_element_type=f32).astype(cdt)
    cidx = lax.broadcasted_iota(jnp.int32, (nh, ncp), 1)
    cmask = cidx < n_cmp
    s_c = jnp.where(cmask, _nt_dot(qk, kvc) + cbias_ref[...], NEG_INF)
    m_c = jnp.max(s_c, axis=-1, keepdims=True)
    p_c = jnp.where(cmask, jnp.exp(s_c - m_c), 0.0)
    p_c = p_c / jnp.sum(p_c, axis=-1, keepdims=True)
    o_c = own_values(jnp.dot(p_c.astype(cdt), kvc, preferred_element_type=f32))

    sum0 = jnp.sum(jnp.where(row < NSA_HPG, p_c, 0.0), axis=0, keepdims=True)
    sum1 = jnp.sum(jnp.where(row < NSA_HPG, 0.0, p_c), axis=0, keepdims=True)
    p_sum = jnp.where(row < NSA_HPG, sum0, sum1)
    nsp = e_ref.shape[0]
    cc = lax.broadcasted_iota(jnp.int32, (ncp, nsp), 0)
    jb = lax.broadcasted_iota(jnp.int32, (ncp, nsp), 1)
    ov = ((cc * CMP_STRIDE < jb * SLC_LEN + SLC_LEN)
          & (cc * CMP_STRIDE + (CMP_LEN - 1) >= jb * SLC_LEN)).astype(cdt)
    p_hi = p_sum.astype(cdt)
    p_lo = (p_sum - p_hi.astype(f32)).astype(cdt)
    imp = jnp.dot(p_hi, ov, preferred_element_type=f32) + jnp.dot(p_lo, ov, preferred_element_type=f32)
    past = n_pages * PAGE_SIZE
    blk = lax.broadcasted_iota(jnp.int32, (nh, nsp), 1)
    cur = past // SLC_LEN
    forced = (blk == 0) | (blk == cur) | (blk == cur - 1)
    imp = jnp.where(forced, FORCE_SCORE, imp)
    imp = jnp.where(blk < ns, imp, -1.0)
    imp_t = imp.T
    bi = lax.broadcasted_iota(jnp.int32, (nsp, nsp), 0)
    bj = lax.broadcasted_iota(jnp.int32, (nsp, nsp), 1)
    sel_rows = []
    for g in range(NSA_KV):
        col = imp_t[:, g * NSA_HPG:g * NSA_HPG + 1]
        rw = imp[g * NSA_HPG:g * NSA_HPG + 1, :]
        ahead = (col > rw) | ((col == rw) & (bi < bj))
        cnt = jnp.sum(jnp.where(ahead, 1.0, 0.0), axis=0, keepdims=True)
        sel_rows.append(jnp.where((cnt < float(min(SLC_TOP, ns))) & (rw >= 0.0), 1.0, 0.0))
    sel = jnp.where(row < NSA_HPG, sel_rows[0], sel_rows[1])

    keys_t = sbuf[slot].astype(cdt)
    picked = jnp.dot(sel.astype(cdt), e_ref[...], preferred_element_type=f32)
    new_s = news_ref[0]
    s_x = jnp.sum(qk.astype(f32) * new_s.astype(cdt).astype(f32), axis=-1, keepdims=True) + b0_ref[...]
    p_s, p_x = _softmax_with_extra(jnp.dot(qk, keys_t, preferred_element_type=f32) + sbias_ref[...],
                                   picked > 0.5, s_x)
    o_s = own_values(_nt_dot(p_s.astype(cdt), keys_t)
                     + p_x.astype(cdt).astype(f32) * new_s.astype(cdt).astype(f32))

    wrows_t = win_ref[0].astype(cdt)
    wb = wrows_t.shape[1]
    new_w = neww_ref[0]
    widx = lax.broadcasted_iota(jnp.int32, (nh, wb), 1)
    s_xw = jnp.sum(qk.astype(f32) * new_w.astype(cdt).astype(f32), axis=-1, keepdims=True) + b0_ref[...]
    p_w, p_xw = _softmax_with_extra(jnp.dot(qk, wrows_t, preferred_element_type=f32) + wbias_ref[...],
                                    wb - widx < WINDOW, s_xw)
    o_w = own_values(_nt_dot(p_w.astype(cdt), wrows_t)
                     + p_xw.astype(cdt).astype(f32) * new_w.astype(cdt).astype(f32))

    gates = gate_ref[0]
    o_ref[0] = gates[:, 0:1] * o_c + gates[:, 1:2] * o_s + gates[:, 2:3] * o_w


def nsa_decode_pallas(q_n, gates, kv_new, c_win, h_all, c_slc, page_table, rel_bias, w2bd, pe_term,
                      cdt=jnp.bfloat16):
    B = q_n.shape[0]
    n_pages = page_table.shape[1]
    past = n_pages * PAGE_SIZE
    wb = c_win.shape[1]
    hpp = PAGE_SIZE // CMP_STRIDE
    ncp = n_pages * hpp
    n_cmp = ncp - (CMP_LEN // CMP_STRIDE - 1)
    ns = -(-(past + 1) // SLC_LEN)
    nsp = -(-ns // 128) * 128
    width = 2 * NSA_KV * HEAD_DIM
    H = NSA_HEADS
    qs = (q_n * HEAD_DIM ** -0.5).reshape(B, NSA_KV, NSA_HPG, HEAD_DIM)
    qk = jnp.einsum('bghd,ge->bghed', qs, jnp.eye(NSA_KV, dtype=qs.dtype)).reshape(B, H, NSA_KV * HEAD_DIM)
    qk = jnp.pad(qk, ((0, 0), (0, 0), (0, width - NSA_KV * HEAD_DIM))).astype(cdt)
    bias_d = rel_bias[rel_bucket(jnp.arange(past + 1, dtype=jnp.int32))].T
    c_end = jnp.arange(ncp) * CMP_STRIDE + (CMP_LEN - 1)
    cbias = bias_d[:, jnp.clip(past - c_end, 0, past)]
    sbias = bias_d[:, past - jnp.arange(past)]
    wbias = bias_d[:, jnp.clip(wb - jnp.arange(wb), 0, past)]
    b0 = bias_d[:, 0:1]
    expand = (jnp.arange(nsp)[:, None] == (jnp.arange(past) // SLC_LEN)[None, :]).astype(cdt)
    const = lambda shape: pl.BlockSpec(shape, lambda b, pt: (0,) * len(shape))
    per_b = lambda shape: pl.BlockSpec((1,) + shape, lambda b, pt: (b,) + (0,) * len(shape))
    grid_spec = pltpu.PrefetchScalarGridSpec(
        num_scalar_prefetch=1,
        grid=(B,),
        in_specs=[per_b((H, width)), per_b((H, 3)), per_b((1, width)), per_b((1, width)), per_b((width, wb)),
                  const((1, pe_term.shape[1])), const(w2bd.shape), const((H, ncp)), const((H, past)),
                  const((H, wb)), const((H, 1)),
                  pl.BlockSpec((nsp, past), lambda b, pt: (0, 0), pipeline_mode=pl.Buffered(1)),
                  pl.BlockSpec(memory_space=pl.ANY), pl.BlockSpec(memory_space=pl.ANY)],
        out_specs=per_b((H, HEAD_DIM)),
        scratch_shapes=[pltpu.VMEM((2, ncp, h_all.shape[2]), jnp.float32),
                        pltpu.VMEM((2, width, past), jnp.float32),
                        pltpu.SemaphoreType.DMA((2, 2))],
    )
    out = pl.pallas_call(
        functools.partial(_nsa_decode_kernel, n_pages=n_pages, n_cmp=n_cmp, ns=ns, cdt=cdt),
        grid_spec=grid_spec,
        out_shape=jax.ShapeDtypeStruct((B, H, HEAD_DIM), jnp.float32),
        compiler_params=pltpu.CompilerParams(dimension_semantics=("arbitrary",), vmem_limit_bytes=VMEM_LIMIT),
        name="nsa_decode",
    )(page_table, qk, gates.reshape(B, H, 3), kv_new[:, 2:4].reshape(B, 1, width),
      kv_new[:, 4:6].reshape(B, 1, width), rows_to_columns(c_win.reshape(B, wb, width)), pe_term, w2bd, cbias,
      sbias, wbias, b0, expand, h_all, rows_to_columns(c_slc.reshape(c_slc.shape[0], PAGE_SIZE, width)))
    return out.reshape(B, H * HEAD_DIM)


def _compress_finish_kernel(h_ref, pe_ref, w2_ref, o_ref):
    hh = h_ref[0]
    nh = hh.shape[0]
    half = hh.shape[1] // 2
    pre = hh[:, :half] + pltpu.roll(hh[:, half:], nh - 1, 0) + pe_ref[...]
    o_ref[0] = jnp.dot(jax.nn.gelu(pre).astype(w2_ref.dtype), w2_ref[...], preferred_element_type=jnp.float32)


def compress_prompt_pallas(cmp_rows, wbd, w2bd, pe_term):
    B, T = cmp_rows.shape[:2]
    nh = T // CMP_STRIDE
    width = 2 * NSA_KV * HEAD_DIM
    rows = B * nh
    tm = next(t for t in (512, 256, 128, 64, 32, 16, 8) if rows % t == 0)
    h = rows_matmul(cmp_rows.reshape(rows, CMP_STRIDE * width), wbd, tm).reshape(B, nh, wbd.shape[1])
    out = pl.pallas_call(
        _compress_finish_kernel,
        grid=(B,),
        in_specs=[pl.BlockSpec((1, nh, wbd.shape[1]), lambda b: (b, 0, 0)),
                  pl.BlockSpec(pe_term.shape, lambda b: (0, 0)),
                  pl.BlockSpec(w2bd.shape, lambda b: (0, 0))],
        out_specs=pl.BlockSpec((1, nh, width), lambda b: (b, 0, 0)),
        out_shape=jax.ShapeDtypeStruct((B, nh, width), jnp.float32),
        compiler_params=pltpu.CompilerParams(dimension_semantics=("parallel",), vmem_limit_bytes=VMEM_LIMIT),
        name="compress_finish",
    )(h, pe_term, w2bd)
    return out.reshape(B, nh, 2, NSA_KV, HEAD_DIM)


SB_CHUNK_PAGES = 2


def _sb_mem_decode_kernel(pt_ref, qs_ref, qm_ref, mem_ref, sb_hbm, os_ref, om_ref, first, extra, sem,
                          *, n_pages, cdt):
    f32 = jnp.float32
    b = pl.program_id(0)
    nb = pl.num_programs(0)
    slot = b % 2
    ch = SB_CHUNK_PAGES
    ck = ch * PAGE_SIZE
    n_chunks = n_pages // ch

    def copies(bb, c, dst, s):
        return [pltpu.make_async_copy(sb_hbm.at[pt_ref[bb, n_pages - (c + 1) * ch + i]],
                                      dst.at[:, pl.ds(i * PAGE_SIZE, PAGE_SIZE)], s) for i in range(ch)]

    @pl.when(b == 0)
    def _():
        for c in copies(0, 0, first.at[0], sem.at[0]):
            c.start()

    @pl.when(b + 1 < nb)
    def _():
        for c in copies(b + 1, 0, first.at[1 - slot], sem.at[1 - slot]):
            c.start()

    for c in copies(b, 0, first.at[slot], sem.at[slot]):
        c.wait()

    qs = qs_ref[0]
    nh = qs.shape[0]
    tri = jnp.where(lax.broadcasted_iota(jnp.int32, (ck, ck), 0) > lax.broadcasted_iota(jnp.int32, (ck, ck), 1),
                    1.0, 0.0).astype(cdt)

    def chunk(rows, carry, acc):
        keys_t = rows.astype(cdt)
        z = jnp.dot(qs, keys_t, preferred_element_type=f32)
        sp = jnp.maximum(z, 0.0) + jnp.log1p(jnp.exp(-jnp.abs(z)))
        l1m = -sp
        suffix = None
        for part in _split3(l1m, cdt):
            d = jnp.dot(part, tri, preferred_element_type=f32)
            suffix = d if suffix is None else suffix + d
        a = jnp.exp(z - sp + suffix + carry)
        acc = acc + _nt_dot(a.astype(cdt), keys_t)
        return carry + suffix[:, 0:1] + l1m[:, 0:1], acc

    carry, acc = chunk(first[slot], jnp.zeros((nh, 1), f32), jnp.zeros((nh, qs.shape[1]), f32))

    def cond(state):
        c, live, _, _ = state
        return (c < n_chunks) & (live > SB_DEAD_LOG)

    def body(state):
        c, _, carry, acc = state
        for cp in copies(b, c, extra, sem.at[2]):
            cp.start()
        for cp in copies(b, c, extra, sem.at[2]):
            cp.wait()
        carry, acc = chunk(extra[...], carry, acc)
        return c + 1, jnp.max(carry), carry, acc

    _, _, _, acc = lax.while_loop(cond, body, (jnp.int32(1), jnp.max(carry), carry, acc))
    v0 = SB_HEADS * HEAD_DIM
    os_ref[0] = jnp.concatenate([acc[h:h + 1, v0 + h * HEAD_DIM:v0 + (h + 1) * HEAD_DIM] for h in range(nh)], axis=0)

    mem = mem_ref[0].astype(cdt)
    s = _nt_dot(qm_ref[0], mem) * (MEM_HD ** -0.5)
    p = jnp.exp(s - jnp.max(s, axis=-1, keepdims=True))
    p = p / jnp.sum(p, axis=-1, keepdims=True)
    om = jnp.dot(p.astype(cdt), mem, preferred_element_type=f32)
    m0 = MEM_HEADS * MEM_HD
    om_ref[0] = jnp.concatenate([om[h:h + 1, m0 + h * MEM_HD:m0 + (h + 1) * MEM_HD] for h in range(MEM_HEADS)], axis=0)


def sb_mem_decode_pallas(q_s, q_m, c_sb, c_mem, page_table, cdt=jnp.bfloat16):
    B = q_s.shape[0]
    n_pages = page_table.shape[1]
    ws = 2 * SB_HEADS * HEAD_DIM
    wm = 2 * MEM_HEADS * MEM_HD
    ml = c_mem.shape[1]
    qs = jnp.einsum('bhd,he->bhed', q_s * HEAD_DIM ** -0.5, jnp.eye(SB_HEADS, dtype=q_s.dtype))
    qs = jnp.pad(qs.reshape(B, SB_HEADS, ws // 2), ((0, 0), (0, 0), (0, ws // 2))).astype(cdt)
    qm = jnp.einsum('bhd,he->bhed', q_m, jnp.eye(MEM_HEADS, dtype=q_m.dtype)).reshape(B, MEM_HEADS, wm // 2)
    qm = jnp.pad(qm, ((0, 0), (0, 8 - MEM_HEADS), (0, wm // 2))).astype(cdt)
    per_b = lambda shape: pl.BlockSpec((1,) + shape, lambda b, pt: (b,) + (0,) * len(shape))
    ck = SB_CHUNK_PAGES * PAGE_SIZE
    grid_spec = pltpu.PrefetchScalarGridSpec(
        num_scalar_prefetch=1,
        grid=(B,),
        in_specs=[per_b((SB_HEADS, ws)), per_b((8, wm)), per_b((ml, wm)), pl.BlockSpec(memory_space=pl.ANY)],
        out_specs=[per_b((SB_HEADS, HEAD_DIM)), per_b((MEM_HEADS, MEM_HD))],
        scratch_shapes=[pltpu.VMEM((2, ws, ck), jnp.float32), pltpu.VMEM((ws, ck), jnp.float32),
                        pltpu.SemaphoreType.DMA((3,))],
    )
    o_s, o_m = pl.pallas_call(
        functools.partial(_sb_mem_decode_kernel, n_pages=n_pages, cdt=cdt),
        grid_spec=grid_spec,
        out_shape=[jax.ShapeDtypeStruct((B, SB_HEADS, HEAD_DIM), jnp.float32),
                   jax.ShapeDtypeStruct((B, MEM_HEADS, MEM_HD), jnp.float32)],
        compiler_params=pltpu.CompilerParams(dimension_semantics=("arbitrary",), vmem_limit_bytes=VMEM_LIMIT),
        name="sb_mem_decode",
    )(page_table, qs, qm, c_mem.reshape(B, ml, wm), rows_to_columns(c_sb.reshape(c_sb.shape[0], PAGE_SIZE, ws)))
    return o_s.reshape(B, SB_HEADS * HEAD_DIM), o_m.reshape(B, MEM_HEADS * MEM_HD)


def rel_bucket(dist):
    n = jnp.maximum(dist, 0)
    max_exact = REL_BUCKETS // 2
    nf = jnp.maximum(n, 1).astype(jnp.float32)
    large = max_exact + (jnp.log(nf / max_exact) / math.log(REL_MAX_DIST / max_exact)
                         * (REL_BUCKETS - max_exact)).astype(jnp.int32)
    return jnp.where(n < max_exact, n, jnp.minimum(large, REL_BUCKETS - 1))


def prompt_layer(x, mem, rel_bias, p):
    B, T = x.shape[:2]
    q_n, kv_n, g_n, qkv_s, q_m, g_m = in_proj(x, p['g_attn'], p['w_in_segs'])
    n_cmp = T // CMP_STRIDE - (CMP_LEN // CMP_STRIDE - 1)
    kvc = compress_prompt_pallas(kv_n[:, :, 0:2], p['cmp_wbd'], p['cmp_w2bd'], p['cmp_pe_term'])[:, :n_cmp]
    att_scale = HEAD_DIM ** -0.5
    o_n = nsa_prompt_pallas(q_n * att_scale, g_n, kvc[:, :, 0], kvc[:, :, 1], kv_n[:, :, 2], kv_n[:, :, 3],
                            kv_n[:, :, 4], kv_n[:, :, 5], rel_bias)
    o_s = sb_prompt_pallas(qkv_s * jnp.array([att_scale, 1.0, 1.0], jnp.float32).reshape(1, 1, 3, 1, 1))
    ml = mem.shape[1]
    mem_rows = B * ml
    tm_mem = 256 if mem_rows % 256 == 0 else mem_rows
    mem_kv = rms_proj(mem.reshape(mem_rows, D_MODEL), p['g_mem'], [p['w_mem_kv'].astype(jnp.bfloat16)],
                      sigmoid=(False,), out_dtypes=(jnp.float32,), tm=tm_mem)[0].reshape(B, ml, -1)
    post = post_attn_pallas(x, o_n, o_s, q_m.reshape(B, T, -1), g_m.reshape(B, T, -1), mem_kv, p, with_mem=True)
    wb = min(WINDOW, T)
    return post, (kv_n[:, :, 0:2], kv_n[:, :, 2:4], qkv_s[:, :, 1:3], kv_n[:, T - wb:, 4:6],
                  mem_kv.reshape(B, ml, 2, MEM_HEADS, MEM_HD))


def sample_layer(x, c_cmp, c_slc, c_sb, c_win, c_mem, page_table, rel_bias, p):
    B, S = x.shape[:2]
    assert S == 1, "the decode kernels handle one new token per sequence"
    q_n, kv_n, g_n, qkv_s, q_m, g_m = in_proj(x, p['g_attn'], p['w_in_segs'])
    win_buf = jnp.concatenate([c_win, kv_n[:, :, 4:6]], axis=1)
    n_phys = c_cmp.shape[0]
    hpp = PAGE_SIZE // CMP_STRIDE
    pages_per_step = next(t for t in (64, 32, 16, 8, 4, 2, 1) if n_phys % t == 0)
    h_all = halves_matmul(c_cmp.reshape(n_phys, PAGE_SIZE, -1), p['cmp_wbd'],
                          pages_per_step).reshape(n_phys, hpp, -1)
    o_n = nsa_decode_pallas(q_n[:, 0], g_n[:, 0], kv_n[:, 0], c_win, h_all, c_slc, page_table, rel_bias,
                            p['cmp_w2bd'], p['cmp_pe_term'])
    o_s, o_m = sb_mem_decode_pallas(qkv_s[:, 0, 0], q_m[:, 0], c_sb, c_mem, page_table)
    post = post_attn_pallas(x.reshape(1, B, D_MODEL), o_n[None], o_s[None], o_m[None], g_m.reshape(1, B, -1),
                            None, p, with_mem=False)
    post = [a.reshape(B, S, -1) for a in post]
    return post, (kv_n[:, :, 0:2], kv_n[:, :, 2:4], qkv_s[:, :, 1:3], win_buf[:, S:])


def kernel(x_prompt, x_sample, mem_prompt, cache_nsa_cmp, cache_nsa_slc, cache_sb, cache_nsa_win,
           cache_mem, page_table, rel_bias, g_attn, w_in, cmp_pe_k, cmp_w1_k, cmp_w2_k, cmp_pe_v,
           cmp_w1_v, cmp_w2_v, g_mem, w_mem_kv, w_br_nsa, w_br_sb, w_br_mem, w_out, g_ffn, w_router,
           b_router, w_up, b_up, w_down, b_down, g_final):
    depth = g_attn.shape[0]
    xp, xs = x_prompt, x_sample
    st_p, st_s = [], []
    for l in range(depth):
        p = {'g_attn': g_attn[l], 'w_in_segs': split_w_in(w_in[l]),
             'cmp_pe_k': cmp_pe_k[l], 'cmp_w1_k': cmp_w1_k[l], 'cmp_w2_k': cmp_w2_k[l],
             'cmp_pe_v': cmp_pe_v[l], 'cmp_w1_v': cmp_w1_v[l], 'cmp_w2_v': cmp_w2_v[l],
             'g_mem': g_mem[l], 'w_mem_kv': w_mem_kv[l],
             'w_br_nsa': w_br_nsa[l], 'w_br_sb': w_br_sb[l], 'w_br_mem': w_br_mem[l], 'w_out': w_out[l],
             'g_ffn': g_ffn[l], 'w_router': w_router[l], 'b_router': b_router[l],
             'w_up': w_up[l], 'b_up': b_up[l], 'w_down': w_down[l], 'b_down': b_down[l]}
        p['cmp_wbd'], p['cmp_w2bd'], p['cmp_pe_term'] = compress_weights(p, jnp.bfloat16)
        xp, sp = prompt_layer(xp, mem_prompt, rel_bias, p)
        xs, ss = sample_layer(xs, cache_nsa_cmp[l], cache_nsa_slc[l], cache_sb[l], cache_nsa_win[l],
                              cache_mem[l], page_table, rel_bias, p)
        xp = ffn_block(*xp, p, g_final, last=(l == depth - 1))
        xs = ffn_block(*xs, p, g_final, last=(l == depth - 1))
        st_p.append(sp)
        st_s.append(ss)
    y_prompt, y_sample = xp, xs

    def stacked(states, i):
        return jnp.stack([s[i] for s in states])

    return (y_prompt, y_sample,
            stacked(st_p, 0), stacked(st_p, 1), stacked(st_p, 2), stacked(st_p, 3), stacked(st_p, 4),
            stacked(st_s, 0), stacked(st_s, 1), stacked(st_s, 2), stacked(st_s, 3))
```
